```python
import jax
import jax.numpy as jnp
from jax import lax
import numpy as np

D_MODEL = 4096
BATCH = 4
SEQ = 2048
DEPTH = 4
DEC_BATCH = 4
DEC_SEQ = 4096
PAST_LEN = 128

D_MIX = D_MODEL
DN_HEAD_DIM = 128
DN_WIDTH = D_MIX // 2
DN_HEADS = DN_WIDTH // DN_HEAD_DIM
DN_CHUNK = 64
QKV_CONV_WIDTH = 3
SG_WIDTH = D_MIX - DN_WIDTH
SG_HEADS = 16
SG_HEAD_DIM = SG_WIDTH // SG_HEADS
SG_CHUNK = 128
D_FF = 11008
FFN_CONV_WIDTH = 3
NORM_EPS = 1e-6
L2_EPS = 1e-6
IN_SPLITS = (DN_WIDTH, DN_WIDTH, DN_WIDTH, DN_WIDTH, 2 * DN_HEADS, 2 * DN_HEADS, SG_WIDTH, SG_WIDTH)
IN_COLS = 4 * DN_WIDTH + 4 * DN_HEADS + 2 * SG_WIDTH

kernel_name = 'hybrid_bidir_deltanet_sgu_encoder'


def rmsnorm(x, g):
    xf = x.astype(jnp.float32)
    y = xf * lax.rsqrt(jnp.mean(xf * xf, axis=-1, keepdims=True) + NORM_EPS)
    return (y * g.astype(jnp.float32)).astype(x.dtype)


def l2norm(x):
    return x * lax.rsqrt(jnp.sum(x * x, axis=-1, keepdims=True) + L2_EPS)


def centred_dwconv(x, w):
    k_width = w.shape[0]
    half = k_width // 2
    s = x.shape[1]
    xp = jnp.pad(x, ((0, 0), (half, half), (0, 0)))
    y = xp[:, 0:s] * w[0]
    for i in range(1, k_width):
        y = y + xp[:, i:i + s] * w[i]
    return y


def gated_delta_chunked(q, k, v, beta, g):
    b, s, h, dk = q.shape
    c = DN_CHUNK
    n = s // c

    def to_chunks(t):
        t = t.reshape((b, n, c, h) + t.shape[3:])
        return jnp.moveaxis(t, 3, 1)

    q = to_chunks(q) * (dk ** -0.5)
    k = to_chunks(k)
    v = to_chunks(v)
    beta = to_chunks(beta)
    g = jnp.cumsum(to_chunks(g), axis=-1)
    incl = jnp.tril(jnp.ones((c, c), dtype=bool))
    strict = jnp.tril(jnp.ones((c, c), dtype=bool), -1)
    decay = jnp.exp(jnp.where(incl, g[..., :, None] - g[..., None, :], -jnp.inf))
    k_beta = k * beta[..., None]
    lower = jnp.where(strict, jnp.einsum('bhncd,bhnsd->bhncs', k_beta, k) * decay, 0.0)
    eye = jnp.eye(c, dtype=q.dtype)
    t_inv = lax.linalg.triangular_solve(eye + lower, jnp.broadcast_to(eye, lower.shape),
                                        left_side=True, lower=True)
    u = jnp.einsum('bhncs,bhnsv->bhncv', t_inv, v * beta[..., None])
    w = jnp.einsum('bhncs,bhnsd->bhncd', t_inv, k_beta * jnp.exp(g)[..., None])
    attn = jnp.where(incl, jnp.einsum('bhncd,bhnsd->bhncs', q, k) * decay, 0.0)
    q_dec = q * jnp.exp(g)[..., None]
    k_dec = k * jnp.exp(g[..., -1:] - g)[..., None]
    chunk_dec = jnp.exp(g[..., -1])

    def step(state, xs):
        u_c, w_c, q_c, a_c, k_c, d_c = xs
        v_new = u_c - jnp.einsum('bhcd,bhdv->bhcv', w_c, state)
        o_c = jnp.einsum('bhcd,bhdv->bhcv', q_c, state) + jnp.einsum('bhcs,bhsv->bhcv', a_c, v_new)
        state = state * d_c[..., None, None] + jnp.einsum('bhcd,bhcv->bhdv', k_c, v_new)
        return state, o_c

    xs = tuple(jnp.moveaxis(t, 2, 0) for t in (u, w, q_dec, attn, k_dec, chunk_dec))
    state0 = jnp.zeros((b, h, dk, v.shape[-1]), dtype=q.dtype)
    _, o = lax.scan(step, state0, xs)
    o = jnp.transpose(o, (1, 0, 3, 2, 4))
    return o.reshape(b, s, h, -1)


def deltanet_group(q, k, v, z, beta_in, a_in, conv_w, a_log, dt_bias, norm_g):
    b, s, _ = q.shape
    out_dtype = q.dtype
    qkv = jax.nn.silu(centred_dwconv(jnp.concatenate([q, k, v], axis=-1), conv_w)).astype(jnp.float32)
    q, k, v = jnp.split(qkv, 3, axis=-1)
    q = l2norm(q.reshape(b, s, DN_HEADS, DN_HEAD_DIM))
    k = l2norm(k.reshape(b, s, DN_HEADS, DN_HEAD_DIM))
    v = v.reshape(b, s, DN_HEADS, DN_HEAD_DIM)
    beta = jax.nn.sigmoid(beta_in.astype(jnp.float32)).reshape(b, s, 2, DN_HEADS)
    g = -jnp.exp(a_log.astype(jnp.float32)) * jax.nn.softplus(
        a_in.astype(jnp.float32).reshape(b, s, 2, DN_HEADS) + dt_bias.astype(jnp.float32))
    o_fwd = gated_delta_chunked(q, k, v, beta[:, :, 0], g[:, :, 0])
    rev = lambda t: jnp.flip(t, axis=1)
    o_bwd = rev(gated_delta_chunked(rev(q), rev(k), rev(v), rev(beta[:, :, 1]), rev(g[:, :, 1])))
    o = rmsnorm(o_fwd + o_bwd, norm_g) * jax.nn.silu(z.astype(jnp.float32).reshape(b, s, DN_HEADS, DN_HEAD_DIM))
    return o.reshape(b, s, DN_WIDTH).astype(out_dtype)


def spatial_gating_group(u, v, norm_g, w_s, b_s):
    b, s, _ = u.shape
    n = s // SG_CHUNK
    u = jax.nn.gelu(u).reshape(b, n, SG_CHUNK, SG_HEADS, SG_HEAD_DIM)
    v = rmsnorm(jax.nn.gelu(v).reshape(b, s, SG_HEADS, SG_HEAD_DIM), norm_g.reshape(SG_HEADS, SG_HEAD_DIM))
    v = v.reshape(b, n, SG_CHUNK, SG_HEADS, SG_HEAD_DIM)
    mixed = jnp.einsum('gpq,bnqgd->bnpgd', w_s, v) + jnp.transpose(b_s)[:, :, None]
    return (u * mixed).reshape(b, s, SG_WIDTH)


def conv_gated_mlp(h, w_up, conv_w, w_down):
    gate, up = jnp.split(h @ w_up, 2, axis=-1)
    gate = centred_dwconv(gate, conv_w)
    return (jax.nn.silu(gate) * up) @ w_down


def trunk(x, norm1_g, w_in, qkv_conv_w, a_log, dt_bias, dn_norm_g, sg_norm_g, sg_w, sg_b,
          w_out, norm2_g, w_up, ffn_conv_w, w_down, final_norm_g):
    split_idx = np.cumsum(np.array(IN_SPLITS))[:-1].tolist()
    for l in range(DEPTH):
        p = rmsnorm(x, norm1_g[l]) @ w_in[l]
        q, k, v, z, beta_in, a_in, sg_u, sg_v = jnp.split(p, split_idx, axis=-1)
        o_dn = deltanet_group(q, k, v, z, beta_in, a_in, qkv_conv_w[l], a_log[l], dt_bias[l], dn_norm_g[l])
        o_sg = spatial_gating_group(sg_u, sg_v, sg_norm_g[l], sg_w[l], sg_b[l])
        x = x + jnp.concatenate([o_dn, o_sg], axis=-1) @ w_out[l]
        x = x + conv_gated_mlp(rmsnorm(x, norm2_g[l]), w_up[l], ffn_conv_w[l], w_down[l])
    return rmsnorm(x, final_norm_g)


def setup_inputs(seed: int = 0) -> dict:
    key = jax.random.key(seed)
    ks = jax.random.split(key, 18)
    f32 = jnp.float32

    def normal(k, shape, scale):
        return jax.random.normal(k, shape, f32) * scale

    x_prompt = normal(ks[0], (BATCH, SEQ, D_MODEL), 1.0)
    x_sample = normal(ks[1], (DEC_BATCH, DEC_SEQ, D_MODEL), 1.0)
    norm1_g = 1.0 + normal(ks[2], (DEPTH, D_MODEL), 0.02)
    w_in = normal(ks[3], (DEPTH, D_MODEL, IN_COLS), D_MODEL ** -0.5)
    qkv_conv_w = normal(ks[4], (DEPTH, QKV_CONV_WIDTH, 3 * DN_WIDTH), QKV_CONV_WIDTH ** -0.5)
    a_log = jnp.log(jax.random.uniform(ks[5], (DEPTH, 2, DN_HEADS), f32, 1.0, 16.0))
    dt = jnp.exp(jax.random.uniform(ks[6], (DEPTH, 2, DN_HEADS), f32,
                                    float(np.log(1e-3)), float(np.log(1e-1))))
    dt_bias = dt + jnp.log(-jnp.expm1(-dt))
    dn_norm_g = 1.0 + normal(ks[7], (DEPTH, DN_HEAD_DIM), 0.02)
    sg_norm_g = 1.0 + normal(ks[8], (DEPTH, SG_WIDTH), 0.02)
    sg_w = normal(ks[9], (DEPTH, SG_HEADS, SG_CHUNK, SG_CHUNK), SG_CHUNK ** -0.5)
    sg_b = 1.0 + normal(ks[10], (DEPTH, SG_HEADS, SG_CHUNK), 0.1)
    w_out = normal(ks[11], (DEPTH, D_MIX, D_MODEL), D_MIX ** -0.5)
    norm2_g = 1.0 + normal(ks[12], (DEPTH, D_MODEL), 0.02)
    w_up = normal(ks[13], (DEPTH, D_MODEL, 2 * D_FF), D_MODEL ** -0.5)
    ffn_conv_w = normal(ks[14], (DEPTH, FFN_CONV_WIDTH, D_FF), FFN_CONV_WIDTH ** -0.5)
    w_down = normal(ks[15], (DEPTH, D_FF, D_MODEL), D_FF ** -0.5)
    final_norm_g = 1.0 + normal(ks[16], (D_MODEL,), 0.02)
    return {'x_prompt': x_prompt, 'x_sample': x_sample, 'norm1_g': norm1_g, 'w_in': w_in,
            'qkv_conv_w': qkv_conv_w, 'a_log': a_log, 'dt_bias': dt_bias, 'dn_norm_g': dn_norm_g,
            'sg_norm_g': sg_norm_g, 'sg_w': sg_w, 'sg_b': sg_b, 'w_out': w_out, 'norm2_g': norm2_g,
            'w_up': w_up, 'ffn_conv_w': ffn_conv_w, 'w_down': w_down, 'final_norm_g': final_norm_g}


def reference(x_prompt, x_sample, norm1_g, w_in, qkv_conv_w, a_log, dt_bias, dn_norm_g, sg_norm_g,
              sg_w, sg_b, w_out, norm2_g, w_up, ffn_conv_w, w_down, final_norm_g):
    y_prompt = trunk(x_prompt, norm1_g, w_in, qkv_conv_w, a_log, dt_bias, dn_norm_g, sg_norm_g, sg_w, sg_b,
                     w_out, norm2_g, w_up, ffn_conv_w, w_down, final_norm_g)
    y_sample = trunk(x_sample, norm1_g, w_in, qkv_conv_w, a_log, dt_bias, dn_norm_g, sg_norm_g, sg_w, sg_b,
                     w_out, norm2_g, w_up, ffn_conv_w, w_down, final_norm_g)
    return (y_prompt, y_sample)
```

```python
import functools
import math

import jax
import jax.numpy as jnp
from jax import lax
from jax.experimental import pallas as pl
from jax.experimental.pallas import tpu as pltpu

NORM_EPS = 1e-6
L2_EPS = 1e-6
LANES = 128
BF16_SUBLANES = 16
DN_CHUNK = 128
LEAF = 16
VMEM_LIMIT = 56 * 1024 * 1024

f32 = jnp.float32
bf16 = jnp.bfloat16


def _pick(n, target, mult):
    best = None
    d = mult
    while d <= min(n, target):
        if n % d == 0:
            best = d
        d += mult
    return best if best is not None else n


def _round_up(n, m):
    return (n + m - 1) // m * m


def _params(sem):
    return pltpu.CompilerParams(dimension_semantics=sem, vmem_limit_bytes=VMEM_LIMIT)


def _bdot(a, b):
    return jnp.dot(a.astype(bf16), b.astype(bf16), preferred_element_type=f32)


def _rmsnorm_kernel(x_ref, g_ref, o_ref):
    x = x_ref[...]
    ms = jnp.mean(x * x, axis=-1, keepdims=True)
    o_ref[...] = (x * lax.rsqrt(ms + NORM_EPS) * g_ref[...]).astype(o_ref.dtype)


def _rmsnorm(x, g, out_dtype):
    t, d = x.shape
    tm = _pick(t, 256, 8)
    return pl.pallas_call(
        _rmsnorm_kernel,
        grid=(t // tm,),
        in_specs=[pl.BlockSpec((tm, d), lambda i: (i, 0)), pl.BlockSpec((1, d), lambda i: (0, 0))],
        out_specs=pl.BlockSpec((tm, d), lambda i: (i, 0)),
        out_shape=jax.ShapeDtypeStruct((t, d), out_dtype),
        compiler_params=_params(("parallel",)),
        name="rmsnorm",
    )(x, g.reshape(1, d).astype(f32))


def _mm_kernel(*refs, nk, has_res):
    if has_res:
        a_ref, w_ref, r_ref, o_ref = refs[:4]
        scratch = refs[4:]
    else:
        a_ref, w_ref, o_ref = refs[:3]
        r_ref = None
        scratch = refs[3:]
    part = jnp.dot(a_ref[...], w_ref[...], preferred_element_type=f32)
    if nk == 1:
        if r_ref is not None:
            part = part + r_ref[...]
        o_ref[...] = part.astype(o_ref.dtype)
        return
    acc_ref = scratch[0]
    k = pl.program_id(2)

    @pl.when(k == 0)
    def _():
        acc_ref[...] = part

    @pl.when(k > 0)
    def _():
        acc_ref[...] += part

    @pl.when(k == nk - 1)
    def _():
        out = acc_ref[...]
        if r_ref is not None:
            out = out + r_ref[...]
        o_ref[...] = out.astype(o_ref.dtype)


def _matmul(a, w, out_dtype, res=None, tm=1024, tn=1024, tk=4096, name="matmul"):
    m, k = a.shape
    n = w.shape[1]
    tm = _pick(m, tm, 8)
    tn = _pick(n, tn, LANES)
    tk = _pick(k, tk, LANES)
    nk = k // tk
    in_specs = [pl.BlockSpec((tm, tk), lambda i, j, kk: (i, kk)),
                pl.BlockSpec((tk, tn), lambda i, j, kk: (kk, j))]
    args = [a, w]
    if res is not None:
        in_specs.append(pl.BlockSpec((tm, tn), lambda i, j, kk: (i, j)))
        args.append(res)
    scratch = [pltpu.VMEM((tm, tn), f32)] if nk > 1 else []
    return pl.pallas_call(
        functools.partial(_mm_kernel, nk=nk, has_res=res is not None),
        grid=(m // tm, n // tn, nk),
        in_specs=in_specs,
        out_specs=pl.BlockSpec((tm, tn), lambda i, j, kk: (i, j)),
        out_shape=jax.ShapeDtypeStruct((m, n), out_dtype),
        scratch_shapes=scratch,
        compiler_params=_params(("parallel", "parallel", "arbitrary")),
        name=name,
    )(*args)


def _shifted(x, prev_row, next_row):
    ts = x.shape[0]
    rows = lax.broadcasted_iota(jnp.int32, x.shape, 0)
    xp = jnp.where(rows == 0, prev_row, pltpu.roll(x, 1, axis=0))
    xn = jnp.where(rows == ts - 1, next_row, pltpu.roll(x, ts - 1, axis=0))
    return xp, xn


def _halo_rows(prev_ref, next_ref, blocks_per_seq):
    i = pl.program_id(0)
    pos = i % blocks_per_seq
    prev = prev_ref[...].astype(f32)[BF16_SUBLANES - 1:BF16_SUBLANES, :]
    nxt = next_ref[...].astype(f32)[0:1, :]
    prev = jnp.where(pos == 0, 0.0, prev)
    nxt = jnp.where(pos == blocks_per_seq - 1, 0.0, nxt)
    return prev, nxt


def _halo_specs(ts, tc, t, col_off_blocks):
    r = ts // BF16_SUBLANES
    last = t // BF16_SUBLANES - 1
    prev = pl.BlockSpec((BF16_SUBLANES, tc), lambda i, j: (jnp.maximum(i * r - 1, 0), j + col_off_blocks))
    nxt = pl.BlockSpec((BF16_SUBLANES, tc), lambda i, j: (jnp.minimum((i + 1) * r, last), j + col_off_blocks))
    return prev, nxt


def _prep_kernel(x_ref, prev_ref, next_ref, cw_ref, *out_refs, blocks_per_seq, normalize, scale, head_dim):
    x = x_ref[...].astype(f32)
    prev, nxt = _halo_rows(prev_ref, next_ref, blocks_per_seq)
    xp, xn = _shifted(x, prev, nxt)
    y = xp * cw_ref[0:1, :] + x * cw_ref[1:2, :] + xn * cw_ref[2:3, :]
    y = y * jax.nn.sigmoid(y)
    o_ref = out_refs[0]
    nheads = y.shape[1] // head_dim
    for hh in range(nheads):
        sl = slice(hh * head_dim, (hh + 1) * head_dim)
        seg = y[:, sl]
        if normalize:
            ss = jnp.sum(seg * seg, axis=-1, keepdims=True)
            seg = seg * lax.rsqrt(ss + L2_EPS)
            if scale != 1.0:
                seg = seg * scale
        o_ref[:, sl] = seg.astype(o_ref.dtype)
        if len(out_refs) > 1:
            out_refs[1][sl, :] = seg.T.astype(out_refs[1].dtype)


def _prep(p, conv_w, col_off, width, seq, head_dim, normalize, scale, transpose_out):
    t = p.shape[0]
    ts = _pick(seq, 512, LANES)
    tc = _pick(width, 512, head_dim)
    off = col_off // tc
    prev_spec, next_spec = _halo_specs(ts, tc, t, off)
    out_shape = [jax.ShapeDtypeStruct((t, width), bf16)]
    out_specs = [pl.BlockSpec((ts, tc), lambda i, j: (i, j))]
    if transpose_out:
        out_shape.append(jax.ShapeDtypeStruct((width, t), bf16))
        out_specs.append(pl.BlockSpec((tc, ts), lambda i, j: (j, i)))
    outs = pl.pallas_call(
        functools.partial(_prep_kernel, blocks_per_seq=seq // ts, normalize=normalize, scale=scale,
                          head_dim=head_dim),
        grid=(t // ts, width // tc),
        in_specs=[pl.BlockSpec((ts, tc), lambda i, j: (i, j + off)), prev_spec, next_spec,
                  pl.BlockSpec((3, tc), lambda i, j: (0, j))],
        out_specs=out_specs,
        out_shape=out_shape,
        compiler_params=_params(("parallel", "parallel")),
        name="qkv_prep",
    )(p, p, p, conv_w)
    return outs


def _tri_inverse(a_mat, masks):
    eye, leaf_mask, off_masks = masks
    ad = jnp.where(leaf_mask, a_mat, 0.0)
    p = -ad
    x = eye + p
    steps = int(math.log2(LEAF)) - 1
    p = _bdot(p, p)
    for _ in range(steps - 1):
        pb = p.astype(bf16)
        r = jnp.dot(pb, jnp.concatenate([pb, x.astype(bf16)], axis=1), preferred_element_type=f32)
        n = p.shape[1]
        p, x = r[:, :n], x + r[:, n:]
    t = x + _bdot(p, x)
    for om in off_masks:
        ao = jnp.where(om, a_mat, 0.0)
        t = t - _bdot(t, _bdot(ao, t))
    return t


def _dn_kernel(q_ref, k_ref, v_ref, kt_ref, z_ref, gate_ref, prm_ref, ng_ref, o_ref,
               gcol_s, grow_s, tinv_s, attn_s, of_s, ob_s, st_s, *, seq, heads):
    c_len = DN_CHUNK
    n_chunks = seq // c_len
    h = pl.program_id(1)
    row = lax.broadcasted_iota(jnp.int32, (c_len, c_len), 0)
    col = lax.broadcasted_iota(jnp.int32, (c_len, c_len), 1)
    eye = (row == col).astype(f32)

    @pl.when(h == 0)
    def _():
        tri_lo = (row >= col).astype(f32)
        tri_up = (row <= col).astype(f32)
        a_row = prm_ref[0:1, :]
        dtb_row = prm_ref[1:2, :]

        def gate_body(c, carry):
            blk = gate_ref[pl.ds(pl.multiple_of(c * c_len, c_len), c_len), :]
            beta = jax.nn.sigmoid(blk)
            g = a_row * jax.nn.softplus(blk + dtb_row)
            g_f = jnp.dot(tri_lo, g, preferred_element_type=f32, precision=lax.Precision.HIGHEST)
            g_b = jnp.dot(tri_up, g, preferred_element_type=f32, precision=lax.Precision.HIGHEST)
            out = jnp.where(col < 2 * heads, beta,
                            jnp.where(col < 3 * heads, g_f, jnp.where(col < 4 * heads, g_b, 0.0)))
            gcol_s[pl.ds(pl.multiple_of(c * c_len, c_len), c_len), :] = out
            grow_s[c] = out.T
            return carry

        lax.fori_loop(0, n_chunks, gate_body, 0)

    leaf_mask = (row // LEAF) == (col // LEAF)
    off_masks = []
    blk = LEAF
    while blk < c_len:
        off_masks.append(((row // (2 * blk)) == (col // (2 * blk))) & ((row // blk) != (col // blk)))
        blk *= 2
    masks = (eye, leaf_mask, off_masks)
    tri_incl = (row >= col, row <= col)

    def gate_forms(c, d):
        cidx = (2 + d) * heads + h
        gblk = gcol_s[pl.ds(pl.multiple_of(c * c_len, c_len), c_len), :]
        gc = jnp.sum(jnp.where(col == cidx, gblk, 0.0), axis=-1, keepdims=True)
        gr = grow_s[c, pl.ds(cidx, 1), :]
        br = grow_s[c, pl.ds(d * heads + h, 1), :]
        return gc, gr, br

    def intra_body(c, carry):
        sl = pl.ds(pl.multiple_of(c * c_len, c_len), c_len)
        qk = jnp.concatenate([q_ref[sl, :], k_ref[sl, :]], axis=0)
        gram = jnp.dot(qk, kt_ref[:, sl], preferred_element_type=f32)
        qk_m, kk_m = gram[:c_len], gram[c_len:]
        for d in range(2):
            gc, gr, br = gate_forms(c, d)
            e = jnp.exp(jnp.where(tri_incl[d], gc - gr, -jnp.inf))
            a_mat = jnp.where(row != col, kk_m * e, 0.0) * br
            attn_s[d, c] = (qk_m * e * br).astype(bf16)
            tinv_s[d, c] = _tri_inverse(a_mat, masks).astype(bf16)
        return carry

    lax.fori_loop(0, n_chunks, intra_body, 0)

    st_s[...] = jnp.zeros_like(st_s)

    def chain_body(i, carry):
        for d, c, o_s, last in ((0, i, of_s, c_len - 1), (1, n_chunks - 1 - i, ob_s, 0)):
            sl = pl.ds(pl.multiple_of(c * c_len, c_len), c_len)
            gc, gr, br = gate_forms(c, d)
            state = st_s[d]
            kq = jnp.concatenate([k_ref[sl, :], q_ref[sl, :]], axis=0)
            kqs = jnp.dot(kq, state.astype(bf16), preferred_element_type=f32)
            eg = jnp.exp(gc)
            r = v_ref[sl, :].astype(f32) - eg * kqs[:c_len]
            y = jnp.dot(tinv_s[d, c], r.astype(bf16), preferred_element_type=f32).astype(bf16)
            o_s[sl, :] = eg * kqs[c_len:] + jnp.dot(attn_s[d, c], y, preferred_element_type=f32)
            g_last = jnp.sum(jnp.where(col[0:1, :] == last, gr, 0.0), axis=-1, keepdims=True)
            dec = jnp.exp(g_last - gr) * br
            kt = (kt_ref[:, sl].astype(f32) * dec).astype(bf16)
            st_s[d] = jnp.exp(g_last) * state + jnp.dot(kt, y, preferred_element_type=f32)
        return carry

    lax.fori_loop(0, n_chunks, chain_body, 0)

    def out_body(c, carry):
        sl = pl.ds(pl.multiple_of(c * c_len, c_len), c_len)
        o = of_s[sl, :] + ob_s[sl, :]
        ms = jnp.mean(o * o, axis=-1, keepdims=True)
        o = o * lax.rsqrt(ms + NORM_EPS) * ng_ref[...]
        z = z_ref[sl, :].astype(f32)
        o_ref[sl, :] = (o * (z * jax.nn.sigmoid(z))).astype(o_ref.dtype)
        return carry

    lax.fori_loop(0, n_chunks, out_body, 0)


def _deltanet(q, k, v, kt, p, z_col_off, gates, prm, norm_g, seq, heads, head_dim):
    t = q.shape[0]
    nb = t // seq
    n_chunks = seq // DN_CHUNK
    zoff = z_col_off // head_dim
    tok = lambda b, h: (b, h)
    return pl.pallas_call(
        functools.partial(_dn_kernel, seq=seq, heads=heads),
        grid=(nb, heads),
        in_specs=[pl.BlockSpec((seq, head_dim), tok), pl.BlockSpec((seq, head_dim), tok),
                  pl.BlockSpec((seq, head_dim), tok),
                  pl.BlockSpec((head_dim, seq), lambda b, h: (h, b)),
                  pl.BlockSpec((seq, head_dim), lambda b, h: (b, h + zoff)),
                  pl.BlockSpec((seq, LANES), lambda b, h: (b, 0)),
                  pl.BlockSpec((2, LANES), lambda b, h: (0, 0)),
                  pl.BlockSpec((1, head_dim), lambda b, h: (0, 0))],
        out_specs=pl.BlockSpec((seq, head_dim), tok),
        out_shape=jax.ShapeDtypeStruct((t, heads * head_dim), bf16),
        scratch_shapes=[pltpu.VMEM((seq, LANES), f32),
                        pltpu.VMEM((n_chunks, LANES, DN_CHUNK), f32),
                        pltpu.VMEM((2, n_chunks, DN_CHUNK, DN_CHUNK), bf16),
                        pltpu.VMEM((2, n_chunks, DN_CHUNK, DN_CHUNK), bf16),
                        pltpu.VMEM((seq, head_dim), f32),
                        pltpu.VMEM((seq, head_dim), f32),
                        pltpu.VMEM((2, head_dim, head_dim), f32)],
        compiler_params=_params(("parallel", "arbitrary")),
        name="deltanet",
    )(q, k, v, kt, p, gates, prm, norm_g)


def _sg_kernel(u_ref, v_ref, ws_ref, bias_ref, ng_ref, o_ref, *, heads, head_dim):
    for g in range(heads):
        sl = slice(g * head_dim, (g + 1) * head_dim)
        v = jax.nn.gelu(v_ref[:, sl].astype(f32))
        ms = jnp.mean(v * v, axis=-1, keepdims=True)
        vn = v * lax.rsqrt(ms + NORM_EPS) * ng_ref[:, sl]
        mixed = jnp.dot(ws_ref[g], vn.astype(bf16), preferred_element_type=f32) + bias_ref[g]
        u = jax.nn.gelu(u_ref[:, sl].astype(f32))
        o_ref[:, sl] = (u * mixed).astype(o_ref.dtype)


def _spatial_gating(p, u_col_off, v_col_off, width, ws, bias, norm_g, heads, chunk):
    t = p.shape[0]
    head_dim = width // heads
    uo, vo = u_col_off // width, v_col_off // width
    return pl.pallas_call(
        functools.partial(_sg_kernel, heads=heads, head_dim=head_dim),
        grid=(t // chunk,),
        in_specs=[pl.BlockSpec((chunk, width), lambda i: (i, uo)),
                  pl.BlockSpec((chunk, width), lambda i: (i, vo)),
                  pl.BlockSpec((heads, chunk, chunk), lambda i: (0, 0, 0)),
                  pl.BlockSpec((heads, chunk, head_dim), lambda i: (0, 0, 0)),
                  pl.BlockSpec((1, width), lambda i: (0, 0))],
        out_specs=pl.BlockSpec((chunk, width), lambda i: (i, 0)),
        out_shape=jax.ShapeDtypeStruct((t, width), bf16),
        compiler_params=_params(("parallel",)),
        name="spatial_gating",
    )(p, p, ws, bias, norm_g)


def _gate_kernel(g_ref, gprev_ref, gnext_ref, u_ref, cw_ref, o_ref, *, blocks_per_seq):
    g = g_ref[...].astype(f32)
    prev, nxt = _halo_rows(gprev_ref, gnext_ref, blocks_per_seq)
    gp, gn = _shifted(g, prev, nxt)
    y = gp * cw_ref[0:1, :] + g * cw_ref[1:2, :] + gn * cw_ref[2:3, :]
    o_ref[...] = (y * jax.nn.sigmoid(y) * u_ref[...].astype(f32)).astype(o_ref.dtype)


def _conv_gate(gu, conv_w, ffp, seq):
    t = gu.shape[0]
    ts = _pick(seq, 512, LANES)
    tc = _pick(ffp, 1024, LANES)
    prev_spec, next_spec = _halo_specs(ts, tc, t, 0)
    uoff = ffp // tc
    return pl.pallas_call(
        functools.partial(_gate_kernel, blocks_per_seq=seq // ts),
        grid=(t // ts, ffp // tc),
        in_specs=[pl.BlockSpec((ts, tc), lambda i, j: (i, j)), prev_spec, next_spec,
                  pl.BlockSpec((ts, tc), lambda i, j: (i, j + uoff)),
                  pl.BlockSpec((3, tc), lambda i, j: (0, j))],
        out_specs=pl.BlockSpec((ts, tc), lambda i, j: (i, j)),
        out_shape=jax.ShapeDtypeStruct((t, ffp), bf16),
        compiler_params=_params(("parallel", "parallel")),
        name="conv_gate",
    )(gu, gu, gu, gu, conv_w)


def _prepare_weights(norm1_g, w_in, qkv_conv_w, a_log, dt_bias, dn_norm_g, sg_norm_g, sg_w, sg_b,
                     w_out, norm2_g, w_up, ffn_conv_w, w_down):
    depth = w_in.shape[0]
    heads = a_log.shape[-1]
    dnw = qkv_conv_w.shape[-1] // 3
    sgw = sg_norm_g.shape[-1]
    ff = ffn_conv_w.shape[-1]
    ffp = _round_up(ff, 1024)
    pad_f = ffp - ff
    g0 = 4 * dnw
    g1 = g0 + 4 * heads
    assert 4 * heads <= LANES
    sg_heads, sg_chunk = sg_w.shape[1], sg_w.shape[2]
    dims = dict(heads=heads, dnw=dnw, sgw=sgw, ffp=ffp, sg_heads=sg_heads, sg_chunk=sg_chunk)
    lead = jnp.zeros((2 * heads,), f32)
    tail = jnp.zeros((LANES - 4 * heads,), f32)
    layers = []
    for l in range(depth):
        wl = w_in[l]
        neg_a = -jnp.exp(a_log[l].astype(f32)).reshape(2 * heads)
        dtb = dt_bias[l].astype(f32).reshape(2 * heads)
        layers.append(dict(
            norm1_g=norm1_g[l],
            w_main=jnp.concatenate([wl[:, :g0], wl[:, g1:]], axis=-1).astype(bf16),
            w_gate=jnp.pad(wl[:, g0:g1], ((0, 0), (0, LANES - 4 * heads))).astype(bf16),
            qkv_cw=qkv_conv_w[l].astype(f32),
            prm=jnp.stack([jnp.concatenate([lead, neg_a, tail]), jnp.concatenate([lead, dtb, tail])]),
            dn_norm_g=dn_norm_g[l].astype(f32).reshape(1, -1),
            sg_norm_g=sg_norm_g[l].astype(f32).reshape(1, -1),
            sg_w=sg_w[l].astype(bf16),
            sg_bias=jnp.broadcast_to(sg_b[l].astype(f32)[..., None], (sg_heads, sg_chunk, sgw // sg_heads)),
            w_out=w_out[l].astype(bf16),
            norm2_g=norm2_g[l],
            w_up=jnp.concatenate([jnp.pad(w_up[l][:, :ff], ((0, 0), (0, pad_f))),
                                  jnp.pad(w_up[l][:, ff:], ((0, 0), (0, pad_f)))], axis=-1).astype(bf16),
            ffn_cw=jnp.pad(ffn_conv_w[l].astype(f32), ((0, 0), (0, pad_f))),
            w_down=jnp.pad(w_down[l], ((0, pad_f), (0, 0))).astype(bf16),
        ))
    return layers, dims


def _trunk(x, layers, dims, final_norm_g):
    nb, seq, d = x.shape
    heads, dnw, sgw, ffp = dims["heads"], dims["dnw"], dims["sgw"], dims["ffp"]
    head_dim = dnw // heads
    xt = x.reshape(nb * seq, d)
    for w in layers:
        hn = _rmsnorm(xt, w["norm1_g"], bf16)
        p = _matmul(hn, w["w_main"], bf16, name="in_proj")
        gates = _matmul(hn, w["w_gate"], f32, name="gate_proj")
        cw = w["qkv_cw"]
        (q,) = _prep(p, cw[:, :dnw], 0, dnw, seq, head_dim, True, head_dim ** -0.5, False)
        k, kt = _prep(p, cw[:, dnw:2 * dnw], dnw, dnw, seq, head_dim, True, 1.0, True)
        (v,) = _prep(p, cw[:, 2 * dnw:], 2 * dnw, dnw, seq, head_dim, False, 1.0, False)
        o_dn = _deltanet(q, k, v, kt, p, 3 * dnw, gates, w["prm"], w["dn_norm_g"], seq, heads, head_dim)
        o_sg = _spatial_gating(p, 4 * dnw, 4 * dnw + sgw, sgw, w["sg_w"], w["sg_bias"], w["sg_norm_g"],
                               dims["sg_heads"], dims["sg_chunk"])
        o = jnp.concatenate([o_dn, o_sg], axis=-1)
        xt = _matmul(o, w["w_out"], f32, res=xt, name="out_proj")
        h2 = _rmsnorm(xt, w["norm2_g"], bf16)
        gu = _matmul(h2, w["w_up"], bf16, name="up_proj")
        hid = _conv_gate(gu, w["ffn_cw"], ffp, seq)
        xt = _matmul(hid, w["w_down"], f32, res=xt, tk=2816, name="down_proj")
    return _rmsnorm(xt, final_norm_g, f32).reshape(nb, seq, d)


def kernel(x_prompt, x_sample, norm1_g, w_in, qkv_conv_w, a_log, dt_bias, dn_norm_g, sg_norm_g, sg_w, sg_b, w_out, norm2_g, w_up, ffn_conv_w, w_down, final_norm_g):
    layers, dims = _prepare_weights(norm1_g, w_in, qkv_conv_w, a_log, dt_bias, dn_norm_g, sg_norm_g, sg_w, sg_b,
                                    w_out, norm2_g, w_up, ffn_conv_w, w_down)
    return (_trunk(x_prompt, layers, dims, final_norm_g), _trunk(x_sample, layers, dims, final_norm_g))
```

```python
import functools
import math

import jax
import jax.numpy as jnp
import numpy as np
from jax import lax
from jax.experimental import pallas as pl
from jax.experimental.pallas import tpu as pltpu

NORM_EPS = 1e-6
L2_EPS = 1e-6
LANES = 128
BF16_SUBLANES = 16
DN_CHUNK = 128
LEAF = 16
DN_HEAD_BLOCK = 8
VMEM_LIMIT = 56 * 1024 * 1024

f32 = jnp.float32
bf16 = jnp.bfloat16


def _pick(n, target, mult):
    best = None
    d = mult
    while d <= min(n, target):
        if n % d == 0:
            best = d
        d += mult
    return best if best is not None else n


def _round_up(n, m):
    return (n + m - 1) // m * m


def _params(sem):
    return pltpu.CompilerParams(dimension_semantics=sem, vmem_limit_bytes=VMEM_LIMIT)


def _bdot(a, b):
    return jnp.dot(a.astype(bf16), b.astype(bf16), preferred_element_type=f32)


def _rmsnorm_kernel(x_ref, g_ref, o_ref):
    x = x_ref[...]
    ms = jnp.mean(x * x, axis=-1, keepdims=True)
    o_ref[...] = (x * lax.rsqrt(ms + NORM_EPS) * g_ref[...]).astype(o_ref.dtype)


def _rmsnorm(x, g, out_dtype):
    t, d = x.shape
    tm = _pick(t, 256, 8)
    return pl.pallas_call(
        _rmsnorm_kernel,
        grid=(t // tm,),
        in_specs=[pl.BlockSpec((tm, d), lambda i: (i, 0)), pl.BlockSpec((1, d), lambda i: (0, 0))],
        out_specs=pl.BlockSpec((tm, d), lambda i: (i, 0)),
        out_shape=jax.ShapeDtypeStruct((t, d), out_dtype),
        compiler_params=_params(("parallel",)),
        name="rmsnorm",
    )(x, g.reshape(1, d).astype(f32))


def _mm_kernel(*refs, nk, has_res):
    if has_res:
        a_ref, w_ref, r_ref, o_ref = refs[:4]
        scratch = refs[4:]
    else:
        a_ref, w_ref, o_ref = refs[:3]
        r_ref = None
        scratch = refs[3:]
    part = jnp.dot(a_ref[...], w_ref[...], preferred_element_type=f32)
    if nk == 1:
        if r_ref is not None:
            part = part + r_ref[...]
        o_ref[...] = part.astype(o_ref.dtype)
        return
    acc_ref = scratch[0]
    k = pl.program_id(2)

    @pl.when(k == 0)
    def _():
        acc_ref[...] = part

    @pl.when(k > 0)
    def _():
        acc_ref[...] += part

    @pl.when(k == nk - 1)
    def _():
        out = acc_ref[...]
        if r_ref is not None:
            out = out + r_ref[...]
        o_ref[...] = out.astype(o_ref.dtype)


def _matmul(a, w, out_dtype, res=None, tm=1024, tn=1024, tk=4096, name="matmul"):
    m, k = a.shape
    n = w.shape[1]
    tm = _pick(m, tm, 8)
    tn = _pick(n, tn, LANES)
    tk = _pick(k, tk, LANES)
    nk = k // tk
    in_specs = [pl.BlockSpec((tm, tk), lambda i, j, kk: (i, kk)),
                pl.BlockSpec((tk, tn), lambda i, j, kk: (kk, j))]
    args = [a, w]
    if res is not None:
        in_specs.append(pl.BlockSpec((tm, tn), lambda i, j, kk: (i, j)))
        args.append(res)
    scratch = [pltpu.VMEM((tm, tn), f32)] if nk > 1 else []
    return pl.pallas_call(
        functools.partial(_mm_kernel, nk=nk, has_res=res is not None),
        grid=(m // tm, n // tn, nk),
        in_specs=in_specs,
        out_specs=pl.BlockSpec((tm, tn), lambda i, j, kk: (i, j)),
        out_shape=jax.ShapeDtypeStruct((m, n), out_dtype),
        scratch_shapes=scratch,
        compiler_params=_params(("parallel", "parallel", "arbitrary")),
        name=name,
    )(*args)


def _shifted(x, prev_row, next_row):
    ts = x.shape[0]
    rows = lax.broadcasted_iota(jnp.int32, x.shape, 0)
    xp = jnp.where(rows == 0, prev_row, pltpu.roll(x, 1, axis=0))
    xn = jnp.where(rows == ts - 1, next_row, pltpu.roll(x, ts - 1, axis=0))
    return xp, xn


def _halo_rows(prev_ref, next_ref, blocks_per_seq):
    i = pl.program_id(0)
    pos = i % blocks_per_seq
    prev = prev_ref[...].astype(f32)[BF16_SUBLANES - 1:BF16_SUBLANES, :]
    nxt = next_ref[...].astype(f32)[0:1, :]
    prev = jnp.where(pos == 0, 0.0, prev)
    nxt = jnp.where(pos == blocks_per_seq - 1, 0.0, nxt)
    return prev, nxt


def _halo_specs(ts, tc, t, col_off_blocks):
    r = ts // BF16_SUBLANES
    last = t // BF16_SUBLANES - 1
    prev = pl.BlockSpec((BF16_SUBLANES, tc), lambda i, j: (jnp.maximum(i * r - 1, 0), j + col_off_blocks))
    nxt = pl.BlockSpec((BF16_SUBLANES, tc), lambda i, j: (jnp.minimum((i + 1) * r, last), j + col_off_blocks))
    return prev, nxt


def _prep_kernel(x_ref, prev_ref, next_ref, cw_ref, *out_refs, blocks_per_seq, normalize, scale, head_dim):
    x = x_ref[...].astype(f32)
    prev, nxt = _halo_rows(prev_ref, next_ref, blocks_per_seq)
    xp, xn = _shifted(x, prev, nxt)
    y = xp * cw_ref[0:1, :] + x * cw_ref[1:2, :] + xn * cw_ref[2:3, :]
    y = y * jax.nn.sigmoid(y)
    o_ref = out_refs[0]
    nheads = y.shape[1] // head_dim
    for hh in range(nheads):
        sl = slice(hh * head_dim, (hh + 1) * head_dim)
        seg = y[:, sl]
        if normalize:
            ss = jnp.sum(seg * seg, axis=-1, keepdims=True)
            seg = seg * lax.rsqrt(ss + L2_EPS)
            if scale != 1.0:
                seg = seg * scale
        o_ref[:, sl] = seg.astype(o_ref.dtype)
        if len(out_refs) > 1:
            out_refs[1][sl, :] = seg.T.astype(out_refs[1].dtype)


def _prep(p, conv_w, col_off, width, seq, head_dim, normalize, scale, transpose_out):
    t = p.shape[0]
    ts = _pick(seq, 512, LANES)
    tc = _pick(width, 512, head_dim)
    off = col_off // tc
    prev_spec, next_spec = _halo_specs(ts, tc, t, off)
    out_shape = [jax.ShapeDtypeStruct((t, width), bf16)]
    out_specs = [pl.BlockSpec((ts, tc), lambda i, j: (i, j))]
    if transpose_out:
        out_shape.append(jax.ShapeDtypeStruct((width, t), bf16))
        out_specs.append(pl.BlockSpec((tc, ts), lambda i, j: (j, i)))
    outs = pl.pallas_call(
        functools.partial(_prep_kernel, blocks_per_seq=seq // ts, normalize=normalize, scale=scale,
                          head_dim=head_dim),
        grid=(t // ts, width // tc),
        in_specs=[pl.BlockSpec((ts, tc), lambda i, j: (i, j + off)), prev_spec, next_spec,
                  pl.BlockSpec((3, tc), lambda i, j: (0, j))],
        out_specs=out_specs,
        out_shape=out_shape,
        compiler_params=_params(("parallel", "parallel")),
        name="qkv_prep",
    )(p, p, p, conv_w)
    return outs


def _gate_prep_kernel(gate_ref, prm_ref, gcol_ref, grow_ref, *, heads):
    c_len = DN_CHUNK
    row = lax.broadcasted_iota(jnp.int32, (c_len, c_len), 0)
    col = lax.broadcasted_iota(jnp.int32, (c_len, c_len), 1)
    blk = gate_ref[...]
    beta = jax.nn.sigmoid(blk)
    g = prm_ref[0:1, :] * jax.nn.softplus(blk + prm_ref[1:2, :])
    g_f = jnp.dot((row >= col).astype(f32), g, preferred_element_type=f32, precision=lax.Precision.HIGHEST)
    g_b = jnp.dot((row <= col).astype(f32), g, preferred_element_type=f32, precision=lax.Precision.HIGHEST)
    out = jnp.where(col < 2 * heads, beta, jnp.where(col < 3 * heads, g_f, jnp.where(col < 4 * heads, g_b, 0.0)))
    gcol_ref[...] = out
    grow_ref[0] = out.T


def _gate_prep(gates, prm, heads):
    t = gates.shape[0]
    n = t // DN_CHUNK
    return pl.pallas_call(
        functools.partial(_gate_prep_kernel, heads=heads),
        grid=(n,),
        in_specs=[pl.BlockSpec((DN_CHUNK, LANES), lambda i: (i, 0)), pl.BlockSpec((2, LANES), lambda i: (0, 0))],
        out_specs=[pl.BlockSpec((DN_CHUNK, LANES), lambda i: (i, 0)),
                   pl.BlockSpec((1, LANES, DN_CHUNK), lambda i: (i, 0, 0))],
        out_shape=[jax.ShapeDtypeStruct((t, LANES), f32), jax.ShapeDtypeStruct((n, LANES, DN_CHUNK), f32)],
        compiler_params=_params(("parallel",)),
        name="gate_prep",
    )(gates, prm)


_M_NEG_LO, _M_NEG_UP, _M_OFFDIAG, _M_EYE, _M_LEAF, _M_OFF0 = 0, 1, 2, 3, 4, 5


def _dn_masks():
    c = DN_CHUNK
    r = np.arange(c)[:, None]
    s = np.arange(c)[None, :]
    ms = [np.where(r >= s, 0.0, -np.inf), np.where(r <= s, 0.0, -np.inf), (r != s) * 1.0, (r == s) * 1.0,
          (r // LEAF == s // LEAF) * 1.0]
    blk = LEAF
    while blk < c:
        ms.append(((r // (2 * blk) == s // (2 * blk)) & (r // blk != s // blk)) * 1.0)
        blk *= 2
    return jnp.asarray(np.stack(ms), f32)


def _tri_inverse(a_mats, m_ref):
    n = a_mats[0].shape[1]
    leaf, eye = m_ref[_M_LEAF], m_ref[_M_EYE]
    ps = [-(a * leaf) for a in a_mats]
    xs = [eye + p for p in ps]
    ps = [_bdot(p, p) for p in ps]
    for _ in range(int(math.log2(LEAF)) - 2):
        pbs = [p.astype(bf16) for p in ps]
        rs = [jnp.dot(pb, jnp.concatenate([pb, x.astype(bf16)], axis=1), preferred_element_type=f32)
              for pb, x in zip(pbs, xs)]
        ps = [r[:, :n] for r in rs]
        xs = [x + r[:, n:] for x, r in zip(xs, rs)]
    ts = [x + _bdot(p, x) for p, x in zip(ps, xs)]
    for lvl in range(m_ref.shape[0] - _M_OFF0):
        off = m_ref[_M_OFF0 + lvl]
        us = [_bdot(a * off, t) for a, t in zip(a_mats, ts)]
        ts = [t - _bdot(t, u) for t, u in zip(ts, us)]
    return ts


def _dn_kernel(qf_ref, kf_ref, vf_ref, ktf_ref, gcf_ref, grf_ref,
               qb_ref, kb_ref, vb_ref, ktb_ref, gcb_ref, grb_ref, m_ref,
               of_ref, ob_ref, st_s, *, heads, head_block, head_dim):
    c_len = DN_CHUNK
    hb = pl.program_id(1)

    @pl.when(pl.program_id(2) == 0)
    def _():
        st_s[...] = jnp.zeros_like(st_s)

    lane = lax.broadcasted_iota(jnp.int32, (c_len, LANES), 1)
    dirs = ((qf_ref, kf_ref, vf_ref, ktf_ref, gcf_ref, grf_ref, of_ref, _M_NEG_LO, c_len - 1),
            (qb_ref, kb_ref, vb_ref, ktb_ref, gcb_ref, grb_ref, ob_ref, _M_NEG_UP, 0))
    chains = [(d, hh) for d in range(2) for hh in range(head_block)]
    sls = [slice(hh * head_dim, (hh + 1) * head_dim) for _, hh in chains]

    q_c = [dirs[d][0][:, sl] for (d, _), sl in zip(chains, sls)]
    k_c = [dirs[d][1][:, sl] for (d, _), sl in zip(chains, sls)]
    kt_c = [dirs[d][3][sl, :] for (d, _), sl in zip(chains, sls)]
    states = [st_s[d, hh] for d, hh in chains]
    gram = [jnp.dot(jnp.concatenate([q, k], axis=0), kt, preferred_element_type=f32)
            for q, k, kt in zip(q_c, k_c, kt_c)]
    kqs = [jnp.dot(jnp.concatenate([k, q], axis=0), s.astype(bf16), preferred_element_type=f32)
           for q, k, s in zip(q_c, k_c, states)]

    gc, gr, br = [], [], []
    for d, hh in chains:
        head = hb * head_block + hh
        g_col = (2 + d) * heads + head
        gc.append(jnp.sum(jnp.where(lane == g_col, dirs[d][4][...], 0.0), axis=-1, keepdims=True))
        gr.append(dirs[d][5][0, pl.ds(g_col, 1), :])
        br.append(dirs[d][5][0, pl.ds(d * heads + head, 1), :])

    e = [jnp.exp(c - r + m_ref[dirs[d][7]]) * b for (d, _), c, r, b in zip(chains, gc, gr, br)]
    attn = [(g[:c_len] * ee).astype(bf16) for g, ee in zip(gram, e)]
    offdiag = m_ref[_M_OFFDIAG]
    tinv = _tri_inverse([g[c_len:] * ee * offdiag for g, ee in zip(gram, e)], m_ref)

    eg = [jnp.exp(c) for c in gc]
    r = [dirs[d][2][:, sl].astype(f32) - ee * ks[:c_len] for (d, _), sl, ee, ks in zip(chains, sls, eg, kqs)]
    y = [jnp.dot(t.astype(bf16), rr.astype(bf16), preferred_element_type=f32).astype(bf16)
         for t, rr in zip(tinv, r)]
    o = [ee * ks[c_len:] + jnp.dot(a, yy, preferred_element_type=f32) for ee, ks, a, yy in zip(eg, kqs, attn, y)]
    for (d, _), sl, oo in zip(chains, sls, o):
        dirs[d][6][:, sl] = oo.astype(dirs[d][6].dtype)
    g_end = [jnp.sum(jnp.where(lane[0:1, :] == dirs[d][8], r_, 0.0), axis=-1, keepdims=True)
             for (d, _), r_ in zip(chains, gr)]
    kt_dec = [(kt.astype(f32) * (jnp.exp(ge - r_) * b)).astype(bf16)
              for kt, ge, r_, b in zip(kt_c, g_end, gr, br)]
    upd = [jnp.dot(kd, yy, preferred_element_type=f32) for kd, yy in zip(kt_dec, y)]
    for (d, hh), ge, s, u in zip(chains, g_end, states, upd):
        st_s[d, hh] = jnp.exp(ge) * s + u


def _deltanet(q, k, v, kt, gcol, grow, seq, heads, head_dim):
    t = q.shape[0]
    nb = t // seq
    n = seq // DN_CHUNK
    hblk = _pick(heads, DN_HEAD_BLOCK, 1)
    w = hblk * head_dim
    fwd = lambda b, h, j: b * n + j
    bwd = lambda b, h, j: b * n + (n - 1 - j)

    def specs(pos):
        tok = pl.BlockSpec((DN_CHUNK, w), lambda b, h, j: (pos(b, h, j), h))
        return [tok, tok, tok,
                pl.BlockSpec((w, DN_CHUNK), lambda b, h, j: (h, pos(b, h, j))),
                pl.BlockSpec((DN_CHUNK, LANES), lambda b, h, j: (pos(b, h, j), 0)),
                pl.BlockSpec((1, LANES, DN_CHUNK), lambda b, h, j: (pos(b, h, j), 0, 0))]

    masks = _dn_masks()
    out_sds = jax.ShapeDtypeStruct((t, heads * head_dim), bf16)
    return pl.pallas_call(
        functools.partial(_dn_kernel, heads=heads, head_block=hblk, head_dim=head_dim),
        grid=(nb, heads // hblk, n),
        in_specs=specs(fwd) + specs(bwd) + [pl.BlockSpec(masks.shape, lambda b, h, j: (0, 0, 0))],
        out_specs=[pl.BlockSpec((DN_CHUNK, w), lambda b, h, j: (fwd(b, h, j), h)),
                   pl.BlockSpec((DN_CHUNK, w), lambda b, h, j: (bwd(b, h, j), h))],
        out_shape=[out_sds, out_sds],
        scratch_shapes=[pltpu.VMEM((2, hblk, head_dim, head_dim), f32)],
        compiler_params=_params(("parallel", "parallel", "arbitrary")),
        name="deltanet",
    )(q, k, v, kt, gcol, grow, q, k, v, kt, gcol, grow, masks)


def _mix_kernel(of_ref, ob_ref, z_ref, dng_ref, u_ref, v_ref, ws_ref, bias_ref, sgg_ref, o_ref, *,
                dn_heads, dn_dim, sg_heads, sg_dim, sg_chunk):
    dnw = dn_heads * dn_dim
    for hh in range(dn_heads):
        sl = slice(hh * dn_dim, (hh + 1) * dn_dim)
        o = of_ref[:, sl].astype(f32) + ob_ref[:, sl].astype(f32)
        ms = jnp.mean(o * o, axis=-1, keepdims=True)
        z = z_ref[:, sl].astype(f32)
        o_ref[:, sl] = (o * lax.rsqrt(ms + NORM_EPS) * dng_ref[...] * (z * jax.nn.sigmoid(z))).astype(o_ref.dtype)
    for c in range(o_ref.shape[0] // sg_chunk):
        rows = slice(c * sg_chunk, (c + 1) * sg_chunk)
        for g in range(sg_heads):
            sl = slice(g * sg_dim, (g + 1) * sg_dim)
            v = jax.nn.gelu(v_ref[rows, sl].astype(f32))
            ms = jnp.mean(v * v, axis=-1, keepdims=True)
            vn = v * lax.rsqrt(ms + NORM_EPS) * sgg_ref[:, sl]
            mixed = jnp.dot(ws_ref[g], vn.astype(bf16), preferred_element_type=f32) + bias_ref[g]
            u = jax.nn.gelu(u_ref[rows, sl].astype(f32))
            o_ref[rows, dnw + g * sg_dim:dnw + (g + 1) * sg_dim] = (u * mixed).astype(o_ref.dtype)


def _mixer_out(o_f, o_b, p, z_col_off, dn_norm_g, dn_heads, u_col_off, v_col_off, sgw, ws, bias, sg_norm_g,
               sg_heads, sg_chunk):
    t, dnw = o_f.shape
    tb = _pick(t, 2 * sg_chunk, sg_chunk)
    zo, uo, vo = z_col_off // dnw, u_col_off // sgw, v_col_off // sgw
    sg_dim = sgw // sg_heads
    return pl.pallas_call(
        functools.partial(_mix_kernel, dn_heads=dn_heads, dn_dim=dnw // dn_heads, sg_heads=sg_heads,
                          sg_dim=sg_dim, sg_chunk=sg_chunk),
        grid=(t // tb,),
        in_specs=[pl.BlockSpec((tb, dnw), lambda i: (i, 0)), pl.BlockSpec((tb, dnw), lambda i: (i, 0)),
                  pl.BlockSpec((tb, dnw), lambda i: (i, zo)),
                  pl.BlockSpec((1, dnw // dn_heads), lambda i: (0, 0)),
                  pl.BlockSpec((tb, sgw), lambda i: (i, uo)), pl.BlockSpec((tb, sgw), lambda i: (i, vo)),
                  pl.BlockSpec((sg_heads, sg_chunk, sg_chunk), lambda i: (0, 0, 0)),
                  pl.BlockSpec((sg_heads, sg_chunk, sg_dim), lambda i: (0, 0, 0)),
                  pl.BlockSpec((1, sgw), lambda i: (0, 0))],
        out_specs=pl.BlockSpec((tb, dnw + sgw), lambda i: (i, 0)),
        out_shape=jax.ShapeDtypeStruct((t, dnw + sgw), bf16),
        compiler_params=_params(("parallel",)),
        name="mixer_out",
    )(o_f, o_b, p, dn_norm_g, p, p, ws, bias, sg_norm_g)


def _gate_kernel(g_ref, gprev_ref, gnext_ref, u_ref, cw_ref, o_ref, *, blocks_per_seq):
    g = g_ref[...].astype(f32)
    prev, nxt = _halo_rows(gprev_ref, gnext_ref, blocks_per_seq)
    gp, gn = _shifted(g, prev, nxt)
    y = gp * cw_ref[0:1, :] + g * cw_ref[1:2, :] + gn * cw_ref[2:3, :]
    o_ref[...] = (y * jax.nn.sigmoid(y) * u_ref[...].astype(f32)).astype(o_ref.dtype)


def _conv_gate(gu, conv_w, ffp, seq):
    t = gu.shape[0]
    ts = _pick(seq, 512, LANES)
    tc = _pick(ffp, 1024, LANES)
    prev_spec, next_spec = _halo_specs(ts, tc, t, 0)
    uoff = ffp // tc
    return pl.pallas_call(
        functools.partial(_gate_kernel, blocks_per_seq=seq // ts),
        grid=(t // ts, ffp // tc),
        in_specs=[pl.BlockSpec((ts, tc), lambda i, j: (i, j)), prev_spec, next_spec,
                  pl.BlockSpec((ts, tc), lambda i, j: (i, j + uoff)),
                  pl.BlockSpec((3, tc), lambda i, j: (0, j))],
        out_specs=pl.BlockSpec((ts, tc), lambda i, j: (i, j)),
        out_shape=jax.ShapeDtypeStruct((t, ffp), bf16),
        compiler_params=_params(("parallel", "parallel")),
        name="conv_gate",
    )(gu, gu, gu, gu, conv_w)


def _prepare_weights(norm1_g, w_in, qkv_conv_w, a_log, dt_bias, dn_norm_g, sg_norm_g, sg_w, sg_b,
                     w_out, norm2_g, w_up, ffn_conv_w, w_down):
    depth = w_in.shape[0]
    heads = a_log.shape[-1]
    dnw = qkv_conv_w.shape[-1] // 3
    sgw = sg_norm_g.shape[-1]
    ff = ffn_conv_w.shape[-1]
    ffp = _round_up(ff, 1024)
    pad_f = ffp - ff
    g0 = 4 * dnw
    g1 = g0 + 4 * heads
    assert 4 * heads <= LANES
    sg_heads, sg_chunk = sg_w.shape[1], sg_w.shape[2]
    dims = dict(heads=heads, dnw=dnw, sgw=sgw, ffp=ffp, sg_heads=sg_heads, sg_chunk=sg_chunk)
    lead = jnp.zeros((2 * heads,), f32)
    tail = jnp.zeros((LANES - 4 * heads,), f32)
    layers = []
    for l in range(depth):
        wl = w_in[l]
        neg_a = -jnp.exp(a_log[l].astype(f32)).reshape(2 * heads)
        dtb = dt_bias[l].astype(f32).reshape(2 * heads)
        layers.append(dict(
            norm1_g=norm1_g[l],
            w_main=jnp.concatenate([wl[:, :g0], wl[:, g1:]], axis=-1).astype(bf16),
            w_gate=jnp.pad(wl[:, g0:g1], ((0, 0), (0, LANES - 4 * heads))).astype(bf16),
            qkv_cw=qkv_conv_w[l].astype(f32),
            prm=jnp.stack([jnp.concatenate([lead, neg_a, tail]), jnp.concatenate([lead, dtb, tail])]),
            dn_norm_g=dn_norm_g[l].astype(f32).reshape(1, -1),
            sg_norm_g=sg_norm_g[l].astype(f32).reshape(1, -1),
            sg_w=sg_w[l].astype(bf16),
            sg_bias=jnp.broadcast_to(sg_b[l].astype(f32)[..., None], (sg_heads, sg_chunk, sgw // sg_heads)),
            w_out=w_out[l].astype(bf16),
            norm2_g=norm2_g[l],
            w_up=jnp.concatenate([jnp.pad(w_up[l][:, :ff], ((0, 0), (0, pad_f))),
                                  jnp.pad(w_up[l][:, ff:], ((0, 0), (0, pad_f)))], axis=-1).astype(bf16),
            ffn_cw=jnp.pad(ffn_conv_w[l].astype(f32), ((0, 0), (0, pad_f))),
            w_down=jnp.pad(w_down[l], ((0, pad_f), (0, 0))).astype(bf16),
        ))
    return layers, dims


def _trunk(x, layers, dims, final_norm_g):
    nb, seq, d = x.shape
    heads, dnw, sgw, ffp = dims["heads"], dims["dnw"], dims["sgw"], dims["ffp"]
    head_dim = dnw // heads
    xt = x.reshape(nb * seq, d)
    for w in layers:
        hn = _rmsnorm(xt, w["norm1_g"], bf16)
        p = _matmul(hn, w["w_main"], bf16, name="in_proj")
        gates = _matmul(hn, w["w_gate"], f32, name="gate_proj")
        cw = w["qkv_cw"]
        (q,) = _prep(p, cw[:, :dnw], 0, dnw, seq, head_dim, True, head_dim ** -0.5, False)
        k, kt = _prep(p, cw[:, dnw:2 * dnw], dnw, dnw, seq, head_dim, True, 1.0, True)
        (v,) = _prep(p, cw[:, 2 * dnw:], 2 * dnw, dnw, seq, head_dim, False, 1.0, False)
        gcol, grow = _gate_prep(gates, w["prm"], heads)
        o_f, o_b = _deltanet(q, k, v, kt, gcol, grow, seq, heads, head_dim)
        o = _mixer_out(o_f, o_b, p, 3 * dnw, w["dn_norm_g"], heads, 4 * dnw, 4 * dnw + sgw, sgw,
                       w["sg_w"], w["sg_bias"], w["sg_norm_g"], dims["sg_heads"], dims["sg_chunk"])
        xt = _matmul(o, w["w_out"], f32, res=xt, name="out_proj")
        h2 = _rmsnorm(xt, w["norm2_g"], bf16)
        gu = _matmul(h2, w["w_up"], bf16, name="up_proj")
        hid = _conv_gate(gu, w["ffn_cw"], ffp, seq)
        xt = _matmul(hid, w["w_down"], f32, res=xt, tk=2816, name="down_proj")
    return _rmsnorm(xt, final_norm_g, f32).reshape(nb, seq, d)


def kernel(x_prompt, x_sample, norm1_g, w_in, qkv_conv_w, a_log, dt_bias, dn_norm_g, sg_norm_g, sg_w, sg_b, w_out, norm2_g, w_up, ffn_conv_w, w_down, final_norm_g):
    layers, dims = _prepare_weights(norm1_g, w_in, qkv_conv_w, a_log, dt_bias, dn_norm_g, sg_norm_g, sg_w, sg_b,
                                    w_out, norm2_g, w_up, ffn_conv_w, w_down)
    return (_trunk(x_prompt, layers, dims, final_norm_g), _trunk(x_sample, layers, dims, final_norm_g))
```

```python
import functools
import math

import jax
import jax.numpy as jnp
import numpy as np
from jax import lax
from jax.experimental import pallas as pl
from jax.experimental.pallas import tpu as pltpu

NORM_EPS = 1e-6
L2_EPS = 1e-6
LANES = 128
BF16_SUBLANES = 16
DN_CHUNK = 128
LEAF = 16
DN_HEAD_BLOCK = 8
VMEM_LIMIT = 56 * 1024 * 1024

f32 = jnp.float32
bf16 = jnp.bfloat16


def _pick(n, target, mult):
    best = None
    d = mult
    while d <= min(n, target):
        if n % d == 0:
            best = d
        d += mult
    return best if best is not None else n


def _round_up(n, m):
    return (n + m - 1) // m * m


def _params(sem):
    return pltpu.CompilerParams(dimension_semantics=sem, vmem_limit_bytes=VMEM_LIMIT)


def _bdot(a, b):
    return jnp.dot(a.astype(bf16), b.astype(bf16), preferred_element_type=f32)


def _rmsnorm_kernel(x_ref, g_ref, o_ref):
    x = x_ref[...]
    ms = jnp.mean(x * x, axis=-1, keepdims=True)
    o_ref[...] = (x * lax.rsqrt(ms + NORM_EPS) * g_ref[...]).astype(o_ref.dtype)


def _rmsnorm(x, g, out_dtype):
    t, d = x.shape
    tm = _pick(t, 256, 8)
    return pl.pallas_call(
        _rmsnorm_kernel,
        grid=(t // tm,),
        in_specs=[pl.BlockSpec((tm, d), lambda i: (i, 0)), pl.BlockSpec((1, d), lambda i: (0, 0))],
        out_specs=pl.BlockSpec((tm, d), lambda i: (i, 0)),
        out_shape=jax.ShapeDtypeStruct((t, d), out_dtype),
        compiler_params=_params(("parallel",)),
        name="rmsnorm",
    )(x, g.reshape(1, d).astype(f32))


def _mm_kernel(*refs, nk, has_res):
    if has_res:
        a_ref, w_ref, r_ref, o_ref = refs[:4]
        scratch = refs[4:]
    else:
        a_ref, w_ref, o_ref = refs[:3]
        r_ref = None
        scratch = refs[3:]
    part = jnp.dot(a_ref[...], w_ref[...], preferred_element_type=f32)
    if nk == 1:
        if r_ref is not None:
            part = part + r_ref[...]
        o_ref[...] = part.astype(o_ref.dtype)
        return
    acc_ref = scratch[0]
    k = pl.program_id(2)

    @pl.when(k == 0)
    def _():
        acc_ref[...] = part

    @pl.when(k > 0)
    def _():
        acc_ref[...] += part

    @pl.when(k == nk - 1)
    def _():
        out = acc_ref[...]
        if r_ref is not None:
            out = out + r_ref[...]
        o_ref[...] = out.astype(o_ref.dtype)


def _matmul(a, w, out_dtype, res=None, tm=1024, tn=1024, tk=4096, col_off=0, n_cols=None, name="matmul"):
    m, k = a.shape
    n = w.shape[1] - col_off if n_cols is None else n_cols
    tm = _pick(m, tm, 8)
    tn = _pick(math.gcd(n, col_off) if col_off else n, tn, LANES)
    tk = _pick(k, tk, LANES)
    nk = k // tk
    off = col_off // tn
    in_specs = [pl.BlockSpec((tm, tk), lambda i, j, kk: (i, kk)),
                pl.BlockSpec((tk, tn), lambda i, j, kk: (kk, j + off))]
    args = [a, w]
    if res is not None:
        in_specs.append(pl.BlockSpec((tm, tn), lambda i, j, kk: (i, j)))
        args.append(res)
    scratch = [pltpu.VMEM((tm, tn), f32)] if nk > 1 else []
    return pl.pallas_call(
        functools.partial(_mm_kernel, nk=nk, has_res=res is not None),
        grid=(m // tm, n // tn, nk),
        in_specs=in_specs,
        out_specs=pl.BlockSpec((tm, tn), lambda i, j, kk: (i, j)),
        out_shape=jax.ShapeDtypeStruct((m, n), out_dtype),
        scratch_shapes=scratch,
        compiler_params=_params(("parallel", "parallel", "arbitrary")),
        name=name,
    )(*args)


def _shifted(x, prev_row, next_row):
    ts = x.shape[0]
    rows = lax.broadcasted_iota(jnp.int32, x.shape, 0)
    xp = jnp.where(rows == 0, prev_row, pltpu.roll(x, 1, axis=0))
    xn = jnp.where(rows == ts - 1, next_row, pltpu.roll(x, ts - 1, axis=0))
    return xp, xn


def _tile_halo_rows(h, tm, seq):
    t, k = h.shape
    nt = t // tm
    h3 = h.reshape(nt, tm, k)
    zero = jnp.zeros((1, k), h.dtype)
    prev = jnp.concatenate([zero, h3[:-1, tm - 1]], axis=0)
    nxt = jnp.concatenate([h3[1:, 0], zero], axis=0)
    pos = jnp.arange(nt) % (seq // tm)
    prev = jnp.where((pos == 0)[:, None], jnp.zeros_like(prev), prev)
    nxt = jnp.where((pos == seq // tm - 1)[:, None], jnp.zeros_like(nxt), nxt)
    pad = jnp.zeros((nt, 6, k), h.dtype)
    return jnp.concatenate([prev[:, None], nxt[:, None], pad], axis=1).reshape(nt * 8, k)


def _proj_conv_kernel(x_ref, w_ref, halo_ref, cw_ref, *out_refs, nsplit, normalize, scale, head_dim):
    x = x_ref[...]
    o_ref = out_refs[0]
    w = o_ref.shape[1] // nsplit
    for s in range(nsplit):
        cs = slice(s * w, (s + 1) * w)
        pre = jnp.dot(x, w_ref[:, cs], preferred_element_type=f32)
        pp, pn = _shifted(pre, halo_ref[0:1, cs], halo_ref[1:2, cs])
        y = pp * cw_ref[0:1, cs] + pre * cw_ref[1:2, cs] + pn * cw_ref[2:3, cs]
        y = y * jax.nn.sigmoid(y)
        for hh in range(w // head_dim):
            seg = y[:, hh * head_dim:(hh + 1) * head_dim]
            if normalize:
                ss = jnp.sum(seg * seg, axis=-1, keepdims=True)
                seg = seg * lax.rsqrt(ss + L2_EPS)
                if scale != 1.0:
                    seg = seg * scale
            hs = slice(s * w + hh * head_dim, s * w + (hh + 1) * head_dim)
            o_ref[:, hs] = seg.astype(o_ref.dtype)
            if len(out_refs) > 1:
                out_refs[1][hs, :] = seg.T.astype(out_refs[1].dtype)


def _proj_conv(h, w, halo, conv_w, col_off, width, tm, head_dim, normalize, scale, transpose_out, tn=1024):
    t, k = h.shape
    tn = _pick(math.gcd(width, col_off) if col_off else width, tn, LANES)
    off = col_off // tn
    out_shape = [jax.ShapeDtypeStruct((t, width), bf16)]
    out_specs = [pl.BlockSpec((tm, tn), lambda i, j: (i, j))]
    if transpose_out:
        out_shape.append(jax.ShapeDtypeStruct((width, t), bf16))
        out_specs.append(pl.BlockSpec((tn, tm), lambda i, j: (j, i)))
    return pl.pallas_call(
        functools.partial(_proj_conv_kernel, nsplit=max(1, tn // 256), normalize=normalize, scale=scale,
                          head_dim=head_dim),
        grid=(t // tm, width // tn),
        in_specs=[pl.BlockSpec((tm, k), lambda i, j: (i, 0)),
                  pl.BlockSpec((k, tn), lambda i, j: (0, j + off)),
                  pl.BlockSpec((8, tn), lambda i, j: (i, j + off)),
                  pl.BlockSpec((3, tn), lambda i, j: (0, j))],
        out_specs=out_specs,
        out_shape=out_shape,
        compiler_params=_params(("parallel", "arbitrary")),
        name="proj_conv",
    )(h, w, halo, conv_w)


def _gate_prep_kernel(gate_ref, prm_ref, gcol_ref, grow_ref, *, heads):
    c_len = DN_CHUNK
    row = lax.broadcasted_iota(jnp.int32, (c_len, c_len), 0)
    col = lax.broadcasted_iota(jnp.int32, (c_len, c_len), 1)
    blk = gate_ref[...]
    beta = jax.nn.sigmoid(blk)
    g = prm_ref[0:1, :] * jax.nn.softplus(blk + prm_ref[1:2, :])
    g_f = jnp.dot((row >= col).astype(f32), g, preferred_element_type=f32, precision=lax.Precision.HIGHEST)
    g_b = jnp.dot((row <= col).astype(f32), g, preferred_element_type=f32, precision=lax.Precision.HIGHEST)
    out = jnp.where(col < 2 * heads, beta, jnp.where(col < 3 * heads, g_f, jnp.where(col < 4 * heads, g_b, 0.0)))
    gcol_ref[...] = out
    grow_ref[0] = out.T


def _gate_prep(gates, prm, heads):
    t = gates.shape[0]
    n = t // DN_CHUNK
    return pl.pallas_call(
        functools.partial(_gate_prep_kernel, heads=heads),
        grid=(n,),
        in_specs=[pl.BlockSpec((DN_CHUNK, LANES), lambda i: (i, 0)), pl.BlockSpec((2, LANES), lambda i: (0, 0))],
        out_specs=[pl.BlockSpec((DN_CHUNK, LANES), lambda i: (i, 0)),
                   pl.BlockSpec((1, LANES, DN_CHUNK), lambda i: (i, 0, 0))],
        out_shape=[jax.ShapeDtypeStruct((t, LANES), f32), jax.ShapeDtypeStruct((n, LANES, DN_CHUNK), f32)],
        compiler_params=_params(("parallel",)),
        name="gate_prep",
    )(gates, prm)


_M_NEG_LO, _M_NEG_UP, _M_OFFDIAG, _M_EYE, _M_LEAF, _M_OFF0 = 0, 1, 2, 3, 4, 5


def _dn_masks():
    c = DN_CHUNK
    r = np.arange(c)[:, None]
    s = np.arange(c)[None, :]
    ms = [np.where(r >= s, 0.0, -np.inf), np.where(r <= s, 0.0, -np.inf), (r != s) * 1.0, (r == s) * 1.0,
          (r // LEAF == s // LEAF) * 1.0]
    blk = LEAF
    while blk < c:
        ms.append(((r // (2 * blk) == s // (2 * blk)) & (r // blk != s // blk)) * 1.0)
        blk *= 2
    return jnp.asarray(np.stack(ms), f32)


def _tri_inverse(a_mats, m_ref):
    n = a_mats[0].shape[1]
    leaf, eye = m_ref[_M_LEAF], m_ref[_M_EYE]
    ps = [-(a * leaf) for a in a_mats]
    xs = [eye + p for p in ps]
    ps = [_bdot(p, p) for p in ps]
    for _ in range(int(math.log2(LEAF)) - 2):
        pbs = [p.astype(bf16) for p in ps]
        rs = [jnp.dot(pb, jnp.concatenate([pb, x.astype(bf16)], axis=1), preferred_element_type=f32)
              for pb, x in zip(pbs, xs)]
        ps = [r[:, :n] for r in rs]
        xs = [x + r[:, n:] for x, r in zip(xs, rs)]
    ts = [x + _bdot(p, x) for p, x in zip(ps, xs)]
    for lvl in range(m_ref.shape[0] - _M_OFF0):
        off = m_ref[_M_OFF0 + lvl]
        us = [_bdot(a * off, t) for a, t in zip(a_mats, ts)]
        ts = [t - _bdot(t, u) for t, u in zip(ts, us)]
    return ts


def _dn_kernel(qf_ref, kf_ref, vf_ref, ktf_ref, gcf_ref, grf_ref,
               qb_ref, kb_ref, vb_ref, ktb_ref, gcb_ref, grb_ref, m_ref,
               of_ref, ob_ref, st_s, *, heads, head_block, head_dim):
    c_len = DN_CHUNK
    hb = pl.program_id(1)

    @pl.when(pl.program_id(2) == 0)
    def _():
        st_s[...] = jnp.zeros_like(st_s)

    lane = lax.broadcasted_iota(jnp.int32, (c_len, LANES), 1)
    dirs = ((qf_ref, kf_ref, vf_ref, ktf_ref, gcf_ref, grf_ref, of_ref, _M_NEG_LO, c_len - 1),
            (qb_ref, kb_ref, vb_ref, ktb_ref, gcb_ref, grb_ref, ob_ref, _M_NEG_UP, 0))
    chains = [(d, hh) for d in range(2) for hh in range(head_block)]
    sls = [slice(hh * head_dim, (hh + 1) * head_dim) for _, hh in chains]

    q_c = [dirs[d][0][:, sl] for (d, _), sl in zip(chains, sls)]
    k_c = [dirs[d][1][:, sl] for (d, _), sl in zip(chains, sls)]
    kt_c = [dirs[d][3][sl, :] for (d, _), sl in zip(chains, sls)]
    states = [st_s[d, hh] for d, hh in chains]
    gram = [jnp.dot(jnp.concatenate([q, k], axis=0), kt, preferred_element_type=f32)
            for q, k, kt in zip(q_c, k_c, kt_c)]
    kqs = [jnp.dot(jnp.concatenate([k, q], axis=0), s.astype(bf16), preferred_element_type=f32)
           for q, k, s in zip(q_c, k_c, states)]

    gc, gr, br = [], [], []
    for d, hh in chains:
        head = hb * head_block + hh
        g_col = (2 + d) * heads + head
        gc.append(jnp.sum(jnp.where(lane == g_col, dirs[d][4][...], 0.0), axis=-1, keepdims=True))
        gr.append(dirs[d][5][0, pl.ds(g_col, 1), :])
        br.append(dirs[d][5][0, pl.ds(d * heads + head, 1), :])

    e = [jnp.exp(c - r + m_ref[dirs[d][7]]) * b for (d, _), c, r, b in zip(chains, gc, gr, br)]
    attn = [(g[:c_len] * ee).astype(bf16) for g, ee in zip(gram, e)]
    offdiag = m_ref[_M_OFFDIAG]
    tinv = _tri_inverse([g[c_len:] * ee * offdiag for g, ee in zip(gram, e)], m_ref)

    eg = [jnp.exp(c) for c in gc]
    r = [dirs[d][2][:, sl].astype(f32) - ee * ks[:c_len] for (d, _), sl, ee, ks in zip(chains, sls, eg, kqs)]
    y = [jnp.dot(t.astype(bf16), rr.astype(bf16), preferred_element_type=f32).astype(bf16)
         for t, rr in zip(tinv, r)]
    o = [ee * ks[c_len:] + jnp.dot(a, yy, preferred_element_type=f32) for ee, ks, a, yy in zip(eg, kqs, attn, y)]
    for (d, _), sl, oo in zip(chains, sls, o):
        dirs[d][6][:, sl] = oo.astype(dirs[d][6].dtype)
    g_end = [jnp.sum(jnp.where(lane[0:1, :] == dirs[d][8], r_, 0.0), axis=-1, keepdims=True)
             for (d, _), r_ in zip(chains, gr)]
    kt_dec = [(kt.astype(f32) * (jnp.exp(ge - r_) * b)).astype(bf16)
              for kt, ge, r_, b in zip(kt_c, g_end, gr, br)]
    upd = [jnp.dot(kd, yy, preferred_element_type=f32) for kd, yy in zip(kt_dec, y)]
    for (d, hh), ge, s, u in zip(chains, g_end, states, upd):
        st_s[d, hh] = jnp.exp(ge) * s + u


def _deltanet(q, k, v, kt, gcol, grow, seq, heads, head_dim):
    t = q.shape[0]
    nb = t // seq
    n = seq // DN_CHUNK
    hblk = _pick(heads, DN_HEAD_BLOCK, 1)
    w = hblk * head_dim
    fwd = lambda b, h, j: b * n + j
    bwd = lambda b, h, j: b * n + (n - 1 - j)

    def specs(pos):
        tok = pl.BlockSpec((DN_CHUNK, w), lambda b, h, j: (pos(b, h, j), h))
        return [tok, tok, tok,
                pl.BlockSpec((w, DN_CHUNK), lambda b, h, j: (h, pos(b, h, j))),
                pl.BlockSpec((DN_CHUNK, LANES), lambda b, h, j: (pos(b, h, j), 0)),
                pl.BlockSpec((1, LANES, DN_CHUNK), lambda b, h, j: (pos(b, h, j), 0, 0))]

    masks = _dn_masks()
    out_sds = jax.ShapeDtypeStruct((t, heads * head_dim), bf16)
    return pl.pallas_call(
        functools.partial(_dn_kernel, heads=heads, head_block=hblk, head_dim=head_dim),
        grid=(nb, heads // hblk, n),
        in_specs=specs(fwd) + specs(bwd) + [pl.BlockSpec(masks.shape, lambda b, h, j: (0, 0, 0))],
        out_specs=[pl.BlockSpec((DN_CHUNK, w), lambda b, h, j: (fwd(b, h, j), h)),
                   pl.BlockSpec((DN_CHUNK, w), lambda b, h, j: (bwd(b, h, j), h))],
        out_shape=[out_sds, out_sds],
        scratch_shapes=[pltpu.VMEM((2, hblk, head_dim, head_dim), f32)],
        compiler_params=_params(("parallel", "parallel", "arbitrary")),
        name="deltanet",
    )(q, k, v, kt, gcol, grow, q, k, v, kt, gcol, grow, masks)


def _mix_kernel(of_ref, ob_ref, z_ref, dng_ref, u_ref, v_ref, ws_ref, bias_ref, sgg_ref, o_ref, *,
                dn_heads, dn_dim, sg_heads, sg_dim, sg_chunk):
    dnw = dn_heads * dn_dim
    for hh in range(dn_heads):
        sl = slice(hh * dn_dim, (hh + 1) * dn_dim)
        o = of_ref[:, sl].astype(f32) + ob_ref[:, sl].astype(f32)
        ms = jnp.mean(o * o, axis=-1, keepdims=True)
        z = z_ref[:, sl].astype(f32)
        o_ref[:, sl] = (o * lax.rsqrt(ms + NORM_EPS) * dng_ref[...] * (z * jax.nn.sigmoid(z))).astype(o_ref.dtype)
    for c in range(o_ref.shape[0] // sg_chunk):
        rows = slice(c * sg_chunk, (c + 1) * sg_chunk)
        for g in range(sg_heads):
            sl = slice(g * sg_dim, (g + 1) * sg_dim)
            v = jax.nn.gelu(v_ref[rows, sl].astype(f32))
            ms = jnp.mean(v * v, axis=-1, keepdims=True)
            vn = v * lax.rsqrt(ms + NORM_EPS) * sgg_ref[:, sl]
            mixed = jnp.dot(ws_ref[g], vn.astype(bf16), preferred_element_type=f32) + bias_ref[g]
            u = jax.nn.gelu(u_ref[rows, sl].astype(f32))
            o_ref[rows, dnw + g * sg_dim:dnw + (g + 1) * sg_dim] = (u * mixed).astype(o_ref.dtype)


def _mixer_out(o_f, o_b, p, z_col_off, dn_norm_g, dn_heads, u_col_off, v_col_off, sgw, ws, bias, sg_norm_g,
               sg_heads, sg_chunk):
    t, dnw = o_f.shape
    tb = _pick(t, 2 * sg_chunk, sg_chunk)
    zo, uo, vo = z_col_off // dnw, u_col_off // sgw, v_col_off // sgw
    sg_dim = sgw // sg_heads
    return pl.pallas_call(
        functools.partial(_mix_kernel, dn_heads=dn_heads, dn_dim=dnw // dn_heads, sg_heads=sg_heads,
                          sg_dim=sg_dim, sg_chunk=sg_chunk),
        grid=(t // tb,),
        in_specs=[pl.BlockSpec((tb, dnw), lambda i: (i, 0)), pl.BlockSpec((tb, dnw), lambda i: (i, 0)),
                  pl.BlockSpec((tb, dnw), lambda i: (i, zo)),
                  pl.BlockSpec((1, dnw // dn_heads), lambda i: (0, 0)),
                  pl.BlockSpec((tb, sgw), lambda i: (i, uo)), pl.BlockSpec((tb, sgw), lambda i: (i, vo)),
                  pl.BlockSpec((sg_heads, sg_chunk, sg_chunk), lambda i: (0, 0, 0)),
                  pl.BlockSpec((sg_heads, sg_chunk, sg_dim), lambda i: (0, 0, 0)),
                  pl.BlockSpec((1, sgw), lambda i: (0, 0))],
        out_specs=pl.BlockSpec((tb, dnw + sgw), lambda i: (i, 0)),
        out_shape=jax.ShapeDtypeStruct((t, dnw + sgw), bf16),
        compiler_params=_params(("parallel",)),
        name="mixer_out",
    )(o_f, o_b, p, dn_norm_g, p, p, ws, bias, sg_norm_g)


def _up_kernel(x_ref, wg_ref, wu_ref, halo_ref, cw_ref, o_ref, *, nsplit):
    x = x_ref[...]
    w = o_ref.shape[1] // nsplit
    for s in range(nsplit):
        cs = slice(s * w, (s + 1) * w)
        g = jnp.dot(x, wg_ref[:, cs], preferred_element_type=f32)
        u = jnp.dot(x, wu_ref[:, cs], preferred_element_type=f32)
        gp, gn = _shifted(g, halo_ref[0:1, cs], halo_ref[1:2, cs])
        y = gp * cw_ref[0:1, cs] + g * cw_ref[1:2, cs] + gn * cw_ref[2:3, cs]
        o_ref[:, cs] = (y * jax.nn.sigmoid(y) * u).astype(o_ref.dtype)


def _up_conv_gate(h, w_up, conv_w, ffp, seq, tm=1024, tn=1024):
    t, k = h.shape
    tm = _pick(seq, tm, BF16_SUBLANES)
    tn = _pick(ffp, tn, LANES)
    halo = _matmul(_tile_halo_rows(h, tm, seq), w_up, f32, n_cols=ffp, name="up_halo")
    uoff = ffp // tn
    return pl.pallas_call(
        functools.partial(_up_kernel, nsplit=max(1, tn // 256)),
        grid=(t // tm, ffp // tn),
        in_specs=[pl.BlockSpec((tm, k), lambda i, j: (i, 0), pipeline_mode=pl.Buffered(1)),
                  pl.BlockSpec((k, tn), lambda i, j: (0, j)),
                  pl.BlockSpec((k, tn), lambda i, j: (0, j + uoff)),
                  pl.BlockSpec((8, tn), lambda i, j: (i, j)),
                  pl.BlockSpec((3, tn), lambda i, j: (0, j))],
        out_specs=pl.BlockSpec((tm, tn), lambda i, j: (i, j)),
        out_shape=jax.ShapeDtypeStruct((t, ffp), bf16),
        compiler_params=_params(("parallel", "arbitrary")),
        name="up_conv_gate",
    )(h, w_up, w_up, halo, conv_w)


def _prepare_weights(norm1_g, w_in, qkv_conv_w, a_log, dt_bias, dn_norm_g, sg_norm_g, sg_w, sg_b,
                     w_out, norm2_g, w_up, ffn_conv_w, w_down):
    depth = w_in.shape[0]
    heads = a_log.shape[-1]
    dnw = qkv_conv_w.shape[-1] // 3
    sgw = sg_norm_g.shape[-1]
    ff = ffn_conv_w.shape[-1]
    ffp = _round_up(ff, 1024)
    pad_f = ffp - ff
    g0 = 4 * dnw
    g1 = g0 + 4 * heads
    assert 4 * heads <= LANES
    sg_heads, sg_chunk = sg_w.shape[1], sg_w.shape[2]
    dims = dict(heads=heads, dnw=dnw, sgw=sgw, ffp=ffp, sg_heads=sg_heads, sg_chunk=sg_chunk)
    lead = jnp.zeros((2 * heads,), f32)
    tail = jnp.zeros((LANES - 4 * heads,), f32)
    layers = []
    for l in range(depth):
        wl = w_in[l]
        neg_a = -jnp.exp(a_log[l].astype(f32)).reshape(2 * heads)
        dtb = dt_bias[l].astype(f32).reshape(2 * heads)
        layers.append(dict(
            norm1_g=norm1_g[l],
            w_main=jnp.concatenate([wl[:, :g0], wl[:, g1:]], axis=-1).astype(bf16),
            w_gate=jnp.pad(wl[:, g0:g1], ((0, 0), (0, LANES - 4 * heads))).astype(bf16),
            qkv_cw=qkv_conv_w[l].astype(f32),
            prm=jnp.stack([jnp.concatenate([lead, neg_a, tail]), jnp.concatenate([lead, dtb, tail])]),
            dn_norm_g=dn_norm_g[l].astype(f32).reshape(1, -1),
            sg_norm_g=sg_norm_g[l].astype(f32).reshape(1, -1),
            sg_w=sg_w[l].astype(bf16),
            sg_bias=jnp.broadcast_to(sg_b[l].astype(f32)[..., None], (sg_heads, sg_chunk, sgw // sg_heads)),
            w_out=w_out[l].astype(bf16),
            norm2_g=norm2_g[l],
            w_up=jnp.concatenate([jnp.pad(w_up[l][:, :ff], ((0, 0), (0, pad_f))),
                                  jnp.pad(w_up[l][:, ff:], ((0, 0), (0, pad_f)))], axis=-1).astype(bf16),
            ffn_cw=jnp.pad(ffn_conv_w[l].astype(f32), ((0, 0), (0, pad_f))),
            w_down=jnp.pad(w_down[l], ((0, pad_f), (0, 0))).astype(bf16),
        ))
    return layers, dims


def _trunk(x, layers, dims, final_norm_g):
    nb, seq, d = x.shape
    heads, dnw, sgw, ffp = dims["heads"], dims["dnw"], dims["sgw"], dims["ffp"]
    head_dim = dnw // heads
    xt = x.reshape(nb * seq, d)
    for w in layers:
        hn = _rmsnorm(xt, w["norm1_g"], bf16)
        gates = _matmul(hn, w["w_gate"], f32, name="gate_proj")
        tm = _pick(seq, 1024, BF16_SUBLANES)
        halo = _matmul(_tile_halo_rows(hn, tm, seq), w["w_main"], f32, n_cols=3 * dnw, name="qkv_halo")
        cw = w["qkv_cw"]
        (q,) = _proj_conv(hn, w["w_main"], halo, cw[:, :dnw], 0, dnw, tm, head_dim, True, head_dim ** -0.5, False)
        k, kt = _proj_conv(hn, w["w_main"], halo, cw[:, dnw:2 * dnw], dnw, dnw, tm, head_dim, True, 1.0, True)
        (v,) = _proj_conv(hn, w["w_main"], halo, cw[:, 2 * dnw:], 2 * dnw, dnw, tm, head_dim, False, 1.0, False)
        p = _matmul(hn, w["w_main"], bf16, col_off=3 * dnw, name="in_proj")
        gcol, grow = _gate_prep(gates, w["prm"], heads)
        o_f, o_b = _deltanet(q, k, v, kt, gcol, grow, seq, heads, head_dim)
        o = _mixer_out(o_f, o_b, p, 0, w["dn_norm_g"], heads, dnw, dnw + sgw, sgw,
                       w["sg_w"], w["sg_bias"], w["sg_norm_g"], dims["sg_heads"], dims["sg_chunk"])
        xt = _matmul(o, w["w_out"], f32, res=xt, name="out_proj")
        h2 = _rmsnorm(xt, w["norm2_g"], bf16)
        hid = _up_conv_gate(h2, w["w_up"], w["ffn_cw"], ffp, seq)
        xt = _matmul(hid, w["w_down"], f32, res=xt, tk=2816, name="down_proj")
    return _rmsnorm(xt, final_norm_g, f32).reshape(nb, seq, d)


def kernel(x_prompt, x_sample, norm1_g, w_in, qkv_conv_w, a_log, dt_bias, dn_norm_g, sg_norm_g, sg_w, sg_b, w_out, norm2_g, w_up, ffn_conv_w, w_down, final_norm_g):
    layers, dims = _prepare_weights(norm1_g, w_in, qkv_conv_w, a_log, dt_bias, dn_norm_g, sg_norm_g, sg_w, sg_b,
                                    w_out, norm2_g, w_up, ffn_conv_w, w_down)
    return (_trunk(x_prompt, layers, dims, final_norm_g), _trunk(x_sample, layers, dims, final_norm_g))
```

```python
import functools
import math

import jax
import jax.numpy as jnp
import numpy as np
from jax import lax
from jax.experimental import pallas as pl
from jax.experimental.pallas import tpu as pltpu

NORM_EPS = 1e-6
L2_EPS = 1e-6
LANES = 128
BF16_SUBLANES = 16
DN_CHUNK = 128
LEAF = 16
DN_HEAD_BLOCK = 8
VMEM_LIMIT = 56 * 1024 * 1024

f32 = jnp.float32
bf16 = jnp.bfloat16


def _pick(n, target, mult):
    best = None
    d = mult
    while d <= min(n, target):
        if n % d == 0:
            best = d
        d += mult
    return best if best is not None else n


def _round_up(n, m):
    return (n + m - 1) // m * m


def _params(sem):
    return pltpu.CompilerParams(dimension_semantics=sem, vmem_limit_bytes=VMEM_LIMIT)


def _bdot(a, b):
    return jnp.dot(a.astype(bf16), b.astype(bf16), preferred_element_type=f32)


def _rmsnorm_kernel(x_ref, g_ref, o_ref):
    x = x_ref[...]
    ms = jnp.mean(x * x, axis=-1, keepdims=True)
    o_ref[...] = (x * lax.rsqrt(ms + NORM_EPS) * g_ref[...]).astype(o_ref.dtype)


def _rmsnorm(x, g, out_dtype):
    t, d = x.shape
    tm = _pick(t, 256, 8)
    return pl.pallas_call(
        _rmsnorm_kernel,
        grid=(t // tm,),
        in_specs=[pl.BlockSpec((tm, d), lambda i: (i, 0)), pl.BlockSpec((1, d), lambda i: (0, 0))],
        out_specs=pl.BlockSpec((tm, d), lambda i: (i, 0)),
        out_shape=jax.ShapeDtypeStruct((t, d), out_dtype),
        compiler_params=_params(("parallel",)),
        name="rmsnorm",
    )(x, g.reshape(1, d).astype(f32))


def _mm_kernel(*refs, nk, has_res):
    if has_res:
        a_ref, w_ref, r_ref, o_ref = refs[:4]
        scratch = refs[4:]
    else:
        a_ref, w_ref, o_ref = refs[:3]
        r_ref = None
        scratch = refs[3:]
    part = jnp.dot(a_ref[...], w_ref[...], preferred_element_type=f32)
    if nk == 1:
        if r_ref is not None:
            part = part + r_ref[...]
        o_ref[...] = part.astype(o_ref.dtype)
        return
    acc_ref = scratch[0]
    k = pl.program_id(2)

    @pl.when(k == 0)
    def _():
        acc_ref[...] = part

    @pl.when(k > 0)
    def _():
        acc_ref[...] += part

    @pl.when(k == nk - 1)
    def _():
        out = acc_ref[...]
        if r_ref is not None:
            out = out + r_ref[...]
        o_ref[...] = out.astype(o_ref.dtype)


def _matmul(a, w, out_dtype, res=None, tm=1024, tn=1024, tk=4096, col_off=0, n_cols=None, name="matmul"):
    m, k = a.shape
    n = w.shape[1] - col_off if n_cols is None else n_cols
    tm = _pick(m, tm, 8)
    tn = _pick(math.gcd(n, col_off) if col_off else n, tn, LANES)
    tk = _pick(k, tk, LANES)
    nk = k // tk
    off = col_off // tn
    in_specs = [pl.BlockSpec((tm, tk), lambda i, j, kk: (i, kk)),
                pl.BlockSpec((tk, tn), lambda i, j, kk: (kk, j + off))]
    args = [a, w]
    if res is not None:
        in_specs.append(pl.BlockSpec((tm, tn), lambda i, j, kk: (i, j)))
        args.append(res)
    scratch = [pltpu.VMEM((tm, tn), f32)] if nk > 1 else []
    return pl.pallas_call(
        functools.partial(_mm_kernel, nk=nk, has_res=res is not None),
        grid=(m // tm, n // tn, nk),
        in_specs=in_specs,
        out_specs=pl.BlockSpec((tm, tn), lambda i, j, kk: (i, j)),
        out_shape=jax.ShapeDtypeStruct((m, n), out_dtype),
        scratch_shapes=scratch,
        compiler_params=_params(("parallel", "parallel", "arbitrary")),
        name=name,
    )(*args)


def _shifted(x, prev_row, next_row):
    ts = x.shape[0]
    rows = lax.broadcasted_iota(jnp.int32, x.shape, 0)
    xp = jnp.where(rows == 0, prev_row, pltpu.roll(x, 1, axis=0))
    xn = jnp.where(rows == ts - 1, next_row, pltpu.roll(x, ts - 1, axis=0))
    return xp, xn


def _tile_halo_rows(h, tm, seq):
    t, k = h.shape
    nt = t // tm
    h3 = h.reshape(nt, tm, k)
    zero = jnp.zeros((1, k), h.dtype)
    prev = jnp.concatenate([zero, h3[:-1, tm - 1]], axis=0)
    nxt = jnp.concatenate([h3[1:, 0], zero], axis=0)
    pos = jnp.arange(nt) % (seq // tm)
    prev = jnp.where((pos == 0)[:, None], jnp.zeros_like(prev), prev)
    nxt = jnp.where((pos == seq // tm - 1)[:, None], jnp.zeros_like(nxt), nxt)
    pad = jnp.zeros((nt, 6, k), h.dtype)
    return jnp.concatenate([prev[:, None], nxt[:, None], pad], axis=1).reshape(nt * 8, k)


def _proj_conv_kernel(x_ref, w_ref, halo_ref, cw_ref, *out_refs, nsplit, normalize, scale, head_dim):
    x = x_ref[...]
    o_ref = out_refs[0]
    w = o_ref.shape[1] // nsplit
    for s in range(nsplit):
        cs = slice(s * w, (s + 1) * w)
        pre = jnp.dot(x, w_ref[:, cs], preferred_element_type=f32)
        pp, pn = _shifted(pre, halo_ref[0:1, cs], halo_ref[1:2, cs])
        y = pp * cw_ref[0:1, cs] + pre * cw_ref[1:2, cs] + pn * cw_ref[2:3, cs]
        y = y * jax.nn.sigmoid(y)
        for hh in range(w // head_dim):
            seg = y[:, hh * head_dim:(hh + 1) * head_dim]
            if normalize:
                ss = jnp.sum(seg * seg, axis=-1, keepdims=True)
                seg = seg * lax.rsqrt(ss + L2_EPS)
                if scale != 1.0:
                    seg = seg * scale
            hs = slice(s * w + hh * head_dim, s * w + (hh + 1) * head_dim)
            o_ref[:, hs] = seg.astype(o_ref.dtype)
            if len(out_refs) > 1:
                out_refs[1][hs, :] = seg.T.astype(out_refs[1].dtype)


def _proj_conv(h, w, halo, conv_w, col_off, width, tm, head_dim, normalize, scale, transpose_out, tn=1024):
    t, k = h.shape
    tn = _pick(math.gcd(width, col_off) if col_off else width, tn, LANES)
    off = col_off // tn
    out_shape = [jax.ShapeDtypeStruct((t, width), bf16)]
    out_specs = [pl.BlockSpec((tm, tn), lambda i, j: (i, j))]
    if transpose_out:
        out_shape.append(jax.ShapeDtypeStruct((width, t), bf16))
        out_specs.append(pl.BlockSpec((tn, tm), lambda i, j: (j, i)))
    return pl.pallas_call(
        functools.partial(_proj_conv_kernel, nsplit=max(1, tn // 256), normalize=normalize, scale=scale,
                          head_dim=head_dim),
        grid=(t // tm, width // tn),
        in_specs=[pl.BlockSpec((tm, k), lambda i, j: (i, 0)),
                  pl.BlockSpec((k, tn), lambda i, j: (0, j + off)),
                  pl.BlockSpec((8, tn), lambda i, j: (i, j + off)),
                  pl.BlockSpec((3, tn), lambda i, j: (0, j))],
        out_specs=out_specs,
        out_shape=out_shape,
        compiler_params=_params(("parallel", "arbitrary")),
        name="proj_conv",
    )(h, w, halo, conv_w)


def _gate_prep_kernel(gate_ref, prm_ref, gcol_ref, grow_ref, *, heads):
    c_len = DN_CHUNK
    row = lax.broadcasted_iota(jnp.int32, (c_len, c_len), 0)
    col = lax.broadcasted_iota(jnp.int32, (c_len, c_len), 1)
    tri_lo = (row >= col).astype(f32)
    tri_up = (row <= col).astype(f32)
    for c in range(gate_ref.shape[0] // c_len):
        rows = slice(c * c_len, (c + 1) * c_len)
        blk = gate_ref[rows, :]
        beta = jax.nn.sigmoid(blk)
        g = prm_ref[0:1, :] * jax.nn.softplus(blk + prm_ref[1:2, :])
        g_f = jnp.dot(tri_lo, g, preferred_element_type=f32, precision=lax.Precision.HIGHEST)
        g_b = jnp.dot(tri_up, g, preferred_element_type=f32, precision=lax.Precision.HIGHEST)
        out = jnp.where(col < 2 * heads, beta, jnp.where(col < 3 * heads, g_f, jnp.where(col < 4 * heads, g_b, 0.0)))
        gcol_ref[rows, :] = out
        grow_ref[c] = out.T


def _gate_prep(gates, prm, heads):
    t = gates.shape[0]
    n = t // DN_CHUNK
    cb = _pick(n, 8, 1)
    return pl.pallas_call(
        functools.partial(_gate_prep_kernel, heads=heads),
        grid=(n // cb,),
        in_specs=[pl.BlockSpec((cb * DN_CHUNK, LANES), lambda i: (i, 0)), pl.BlockSpec((2, LANES), lambda i: (0, 0))],
        out_specs=[pl.BlockSpec((cb * DN_CHUNK, LANES), lambda i: (i, 0)),
                   pl.BlockSpec((cb, LANES, DN_CHUNK), lambda i: (i, 0, 0))],
        out_shape=[jax.ShapeDtypeStruct((t, LANES), f32), jax.ShapeDtypeStruct((n, LANES, DN_CHUNK), f32)],
        compiler_params=_params(("parallel",)),
        name="gate_prep",
    )(gates, prm)


_M_NEG_LO, _M_NEG_UP, _M_OFFDIAG, _M_EYE, _M_LEAF, _M_OFF0 = 0, 1, 2, 3, 4, 5


def _dn_masks():
    c = DN_CHUNK
    r = np.arange(c)[:, None]
    s = np.arange(c)[None, :]
    ms = [np.where(r >= s, 0.0, -np.inf), np.where(r <= s, 0.0, -np.inf), (r != s) * 1.0, (r == s) * 1.0,
          (r // LEAF == s // LEAF) * 1.0]
    blk = LEAF
    while blk < c:
        ms.append(((r // (2 * blk) == s // (2 * blk)) & (r // blk != s // blk)) * 1.0)
        blk *= 2
    return jnp.asarray(np.stack(ms), f32)


def _block_rows(x, blk, parity):
    return jnp.concatenate([x[i * blk:(i + 1) * blk] for i in range(x.shape[0] // blk) if i % 2 == parity], axis=0)


def _merge_block_rows(base, rows, blk, parity, combine):
    out = []
    for i in range(base.shape[0] // blk):
        piece = base[i * blk:(i + 1) * blk]
        if i % 2 == parity:
            piece = combine(piece, rows[(i // 2) * blk:(i // 2 + 1) * blk])
        out.append(piece)
    return jnp.concatenate(out, axis=0)


def _tri_inverse(a_mats, lower, m_ref):
    n = a_mats[0].shape[1]
    leaf, eye = m_ref[_M_LEAF], m_ref[_M_EYE]
    ps = [-(a * leaf) for a in a_mats]
    xs = [eye + p for p in ps]
    ps = [_bdot(p, p) for p in ps]
    for _ in range(int(math.log2(LEAF)) - 2):
        pbs = [p.astype(bf16) for p in ps]
        rs = [jnp.dot(pb, jnp.concatenate([pb, x.astype(bf16)], axis=1), preferred_element_type=f32)
              for pb, x in zip(pbs, xs)]
        ps = [r[:, :n] for r in rs]
        xs = [x + r[:, n:] for x, r in zip(xs, rs)]
    ts = [x + _bdot(p, x) for p, x in zip(ps, xs)]
    parity = [1 if lo else 0 for lo in lower]
    blk = LEAF
    for lvl in range(m_ref.shape[0] - _M_OFF0):
        off = m_ref[_M_OFF0 + lvl]
        zeros = jnp.zeros((n, n), bf16)
        vs = [_bdot(_block_rows(a * off, blk, pr), t).astype(bf16) for a, t, pr in zip(a_mats, ts, parity)]
        v_full = [_merge_block_rows(zeros, v, blk, pr, lambda _, r: r) for v, pr in zip(vs, parity)]
        ds = [_bdot(_block_rows(t, blk, pr), vf) for t, vf, pr in zip(ts, v_full, parity)]
        ts = [_merge_block_rows(t, d, blk, pr, lambda p, r: p - r) for t, d, pr in zip(ts, ds, parity)]
        blk *= 2
    return ts


def _dn_kernel(qf_ref, kf_ref, vf_ref, ktf_ref, gcf_ref, grf_ref,
               qb_ref, kb_ref, vb_ref, ktb_ref, gcb_ref, grb_ref, m_ref,
               of_ref, ob_ref, st_s, *, heads, head_block, head_dim):
    c_len = DN_CHUNK
    hb = pl.program_id(1)

    @pl.when(pl.program_id(2) == 0)
    def _():
        st_s[...] = jnp.zeros_like(st_s)

    lane = lax.broadcasted_iota(jnp.int32, (c_len, LANES), 1)
    dirs = ((qf_ref, kf_ref, vf_ref, ktf_ref, gcf_ref, grf_ref, of_ref, _M_NEG_LO, c_len - 1),
            (qb_ref, kb_ref, vb_ref, ktb_ref, gcb_ref, grb_ref, ob_ref, _M_NEG_UP, 0))
    chains = [(d, hh) for d in range(2) for hh in range(head_block)]
    sls = [slice(hh * head_dim, (hh + 1) * head_dim) for _, hh in chains]

    q_c = [dirs[d][0][:, sl] for (d, _), sl in zip(chains, sls)]
    k_c = [dirs[d][1][:, sl] for (d, _), sl in zip(chains, sls)]
    kt_c = [dirs[d][3][sl, :] for (d, _), sl in zip(chains, sls)]
    states = [st_s[d, hh] for d, hh in chains]
    gram = [jnp.dot(jnp.concatenate([q, k], axis=0), kt, preferred_element_type=f32)
            for q, k, kt in zip(q_c, k_c, kt_c)]
    kqs = [jnp.dot(jnp.concatenate([k, q], axis=0), s.astype(bf16), preferred_element_type=f32)
           for q, k, s in zip(q_c, k_c, states)]

    gc, gr, br = [], [], []
    for d, hh in chains:
        head = hb * head_block + hh
        g_col = (2 + d) * heads + head
        gc.append(jnp.sum(jnp.where(lane == g_col, dirs[d][4][...], 0.0), axis=-1, keepdims=True))
        gr.append(dirs[d][5][0, pl.ds(g_col, 1), :])
        br.append(dirs[d][5][0, pl.ds(d * heads + head, 1), :])

    e = [jnp.exp(c - r + m_ref[dirs[d][7]]) * b for (d, _), c, r, b in zip(chains, gc, gr, br)]
    attn = [(g[:c_len] * ee).astype(bf16) for g, ee in zip(gram, e)]
    offdiag = m_ref[_M_OFFDIAG]
    tinv = _tri_inverse([g[c_len:] * ee * offdiag for g, ee in zip(gram, e)], [d == 0 for d, _ in chains], m_ref)

    eg = [jnp.exp(c) for c in gc]
    r = [dirs[d][2][:, sl].astype(f32) - ee * ks[:c_len] for (d, _), sl, ee, ks in zip(chains, sls, eg, kqs)]
    y = [jnp.dot(t.astype(bf16), rr.astype(bf16), preferred_element_type=f32).astype(bf16)
         for t, rr in zip(tinv, r)]
    g_end = [jnp.sum(jnp.where(lane[0:1, :] == dirs[d][8], r_, 0.0), axis=-1, keepdims=True)
             for (d, _), r_ in zip(chains, gr)]
    kt_dec = [(kt.astype(f32) * (jnp.exp(ge - r_) * b)).astype(bf16)
              for kt, ge, r_, b in zip(kt_c, g_end, gr, br)]
    ay = [jnp.dot(jnp.concatenate([a, kd], axis=0), yy, preferred_element_type=f32)
          for a, kd, yy in zip(attn, kt_dec, y)]
    for (d, hh), sl, ee, ks, ay_, ge, s in zip(chains, sls, eg, kqs, ay, g_end, states):
        dirs[d][6][:, sl] = (ee * ks[c_len:] + ay_[:c_len]).astype(dirs[d][6].dtype)
        st_s[d, hh] = jnp.exp(ge) * s + ay_[c_len:]


def _deltanet(q, k, v, kt, gcol, grow, seq, heads, head_dim):
    t = q.shape[0]
    nb = t // seq
    n = seq // DN_CHUNK
    hblk = _pick(heads, DN_HEAD_BLOCK, 1)
    w = hblk * head_dim
    fwd = lambda b, h, j: b * n + j
    bwd = lambda b, h, j: b * n + (n - 1 - j)

    def specs(pos):
        tok = pl.BlockSpec((DN_CHUNK, w), lambda b, h, j: (pos(b, h, j), h))
        return [tok, tok, tok,
                pl.BlockSpec((w, DN_CHUNK), lambda b, h, j: (h, pos(b, h, j))),
                pl.BlockSpec((DN_CHUNK, LANES), lambda b, h, j: (pos(b, h, j), 0)),
                pl.BlockSpec((1, LANES, DN_CHUNK), lambda b, h, j: (pos(b, h, j), 0, 0))]

    masks = _dn_masks()
    out_sds = jax.ShapeDtypeStruct((t, heads * head_dim), bf16)
    return pl.pallas_call(
        functools.partial(_dn_kernel, heads=heads, head_block=hblk, head_dim=head_dim),
        grid=(nb, heads // hblk, n),
        in_specs=specs(fwd) + specs(bwd) + [pl.BlockSpec(masks.shape, lambda b, h, j: (0, 0, 0))],
        out_specs=[pl.BlockSpec((DN_CHUNK, w), lambda b, h, j: (fwd(b, h, j), h)),
                   pl.BlockSpec((DN_CHUNK, w), lambda b, h, j: (bwd(b, h, j), h))],
        out_shape=[out_sds, out_sds],
        scratch_shapes=[pltpu.VMEM((2, hblk, head_dim, head_dim), f32)],
        compiler_params=_params(("parallel", "parallel", "arbitrary")),
        name="deltanet",
    )(q, k, v, kt, gcol, grow, q, k, v, kt, gcol, grow, masks)


def _mix_kernel(of_ref, ob_ref, z_ref, dng_ref, u_ref, v_ref, ws_ref, bias_ref, sgg_ref, o_ref, *,
                dn_heads, dn_dim, sg_heads, sg_dim, sg_chunk):
    dnw = dn_heads * dn_dim
    for hh in range(dn_heads):
        sl = slice(hh * dn_dim, (hh + 1) * dn_dim)
        o = of_ref[:, sl].astype(f32) + ob_ref[:, sl].astype(f32)
        ms = jnp.mean(o * o, axis=-1, keepdims=True)
        z = z_ref[:, sl].astype(f32)
        o_ref[:, sl] = (o * lax.rsqrt(ms + NORM_EPS) * dng_ref[...] * (z * jax.nn.sigmoid(z))).astype(o_ref.dtype)
    for c in range(o_ref.shape[0] // sg_chunk):
        rows = slice(c * sg_chunk, (c + 1) * sg_chunk)
        for g in range(sg_heads):
            sl = slice(g * sg_dim, (g + 1) * sg_dim)
            v = jax.nn.gelu(v_ref[rows, sl].astype(f32))
            ms = jnp.mean(v * v, axis=-1, keepdims=True)
            vn = v * lax.rsqrt(ms + NORM_EPS) * sgg_ref[:, sl]
            mixed = jnp.dot(ws_ref[g], vn.astype(bf16), preferred_element_type=f32) + bias_ref[g]
            u = jax.nn.gelu(u_ref[rows, sl].astype(f32))
            o_ref[rows, dnw + g * sg_dim:dnw + (g + 1) * sg_dim] = (u * mixed).astype(o_ref.dtype)


def _mixer_out(o_f, o_b, z, dn_norm_g, dn_heads, p_sg, sgw, ws, bias, sg_norm_g, sg_heads, sg_chunk):
    t, dnw = o_f.shape
    tb = _pick(t, 2 * sg_chunk, sg_chunk)
    sg_dim = sgw // sg_heads
    return pl.pallas_call(
        functools.partial(_mix_kernel, dn_heads=dn_heads, dn_dim=dnw // dn_heads, sg_heads=sg_heads,
                          sg_dim=sg_dim, sg_chunk=sg_chunk),
        grid=(t // tb,),
        in_specs=[pl.BlockSpec((tb, dnw), lambda i: (i, 0)), pl.BlockSpec((tb, dnw), lambda i: (i, 0)),
                  pl.BlockSpec((tb, dnw), lambda i: (i, 0)),
                  pl.BlockSpec((1, dnw // dn_heads), lambda i: (0, 0)),
                  pl.BlockSpec((tb, sgw), lambda i: (i, 0)), pl.BlockSpec((tb, sgw), lambda i: (i, 1)),
                  pl.BlockSpec((sg_heads, sg_chunk, sg_chunk), lambda i: (0, 0, 0)),
                  pl.BlockSpec((sg_heads, sg_chunk, sg_dim), lambda i: (0, 0, 0)),
                  pl.BlockSpec((1, sgw), lambda i: (0, 0))],
        out_specs=pl.BlockSpec((tb, dnw + sgw), lambda i: (i, 0)),
        out_shape=jax.ShapeDtypeStruct((t, dnw + sgw), bf16),
        compiler_params=_params(("parallel",)),
        name="mixer_out",
    )(o_f, o_b, z, dn_norm_g, p_sg, p_sg, ws, bias, sg_norm_g)


def _up_kernel(x_ref, wg_ref, wu_ref, halo_ref, cw_ref, o_ref, *, nsplit):
    x = x_ref[...]
    w = o_ref.shape[1] // nsplit
    for s in range(nsplit):
        cs = slice(s * w, (s + 1) * w)
        g = jnp.dot(x, wg_ref[:, cs], preferred_element_type=f32)
        u = jnp.dot(x, wu_ref[:, cs], preferred_element_type=f32)
        gp, gn = _shifted(g, halo_ref[0:1, cs], halo_ref[1:2, cs])
        y = gp * cw_ref[0:1, cs] + g * cw_ref[1:2, cs] + gn * cw_ref[2:3, cs]
        o_ref[:, cs] = (y * jax.nn.sigmoid(y) * u).astype(o_ref.dtype)


def _up_conv_gate(h, w_up, conv_w, ffp, seq, tm=1024, tn=1024):
    t, k = h.shape
    tm = _pick(seq, tm, BF16_SUBLANES)
    tn = _pick(ffp, tn, LANES)
    halo = _matmul(_tile_halo_rows(h, tm, seq), w_up, f32, n_cols=ffp, name="up_halo")
    uoff = ffp // tn
    return pl.pallas_call(
        functools.partial(_up_kernel, nsplit=max(1, tn // 256)),
        grid=(t // tm, ffp // tn),
        in_specs=[pl.BlockSpec((tm, k), lambda i, j: (i, 0), pipeline_mode=pl.Buffered(1)),
                  pl.BlockSpec((k, tn), lambda i, j: (0, j)),
                  pl.BlockSpec((k, tn), lambda i, j: (0, j + uoff)),
                  pl.BlockSpec((8, tn), lambda i, j: (i, j)),
                  pl.BlockSpec((3, tn), lambda i, j: (0, j))],
        out_specs=pl.BlockSpec((tm, tn), lambda i, j: (i, j)),
        out_shape=jax.ShapeDtypeStruct((t, ffp), bf16),
        compiler_params=_params(("parallel", "arbitrary")),
        name="up_conv_gate",
    )(h, w_up, w_up, halo, conv_w)


def _cast_kernel(x_ref, o_ref, *, zero_blocks, axis):
    blk = pl.program_id(axis)
    is_zero = functools.reduce(jnp.logical_or, [blk == z for z in zero_blocks], False)
    o_ref[...] = jnp.where(is_zero, 0.0, x_ref[...]).astype(o_ref.dtype)


def _cast_pad(w, layer, axis, segments, pad_to=None):
    _, rows, cols = w.shape
    blk = functools.reduce(math.gcd, [v for seg in segments for v in seg if v] + [1024])
    src_of, zero_blocks = [], []
    for start, length in segments:
        padded = _round_up(length, pad_to) if pad_to else length
        for b in range(padded // blk):
            if b * blk < length:
                src_of.append((start + b * blk) // blk)
            else:
                zero_blocks.append(len(src_of))
                src_of.append(src_of[-1])
    nblk = len(src_of)
    table = jnp.asarray(src_of, jnp.int32)
    if axis == 1:
        tr = _pick(rows, 1024, 8)
        grid = (rows // tr, nblk)
        in_spec = pl.BlockSpec((None, tr, blk), lambda i, j, tab: (layer, i, tab[j]))
        out_spec = pl.BlockSpec((tr, blk), lambda i, j, tab: (i, j))
        out_shape = (rows, nblk * blk)
    else:
        grid = (1, nblk)
        in_spec = pl.BlockSpec((None, blk, cols), lambda i, j, tab: (layer, tab[j], 0))
        out_spec = pl.BlockSpec((blk, cols), lambda i, j, tab: (j, 0))
        out_shape = (nblk * blk, cols)

    def body(tab_ref, x_ref, o_ref):
        _cast_kernel(x_ref, o_ref, zero_blocks=zero_blocks, axis=1)

    return pl.pallas_call(
        body,
        grid_spec=pltpu.PrefetchScalarGridSpec(num_scalar_prefetch=1, grid=grid, in_specs=[in_spec],
                                               out_specs=out_spec),
        out_shape=jax.ShapeDtypeStruct(out_shape, bf16),
        compiler_params=_params(("parallel", "parallel")),
        name="cast_pad",
    )(table, w)


def _shift_cast_kernel(a_ref, b_ref, c_ref, o_ref, *, shift):
    x = jnp.concatenate([a_ref[...], b_ref[...], c_ref[...]], axis=1)
    o_ref[...] = x[:, shift:shift + o_ref.shape[1]].astype(o_ref.dtype)


def _shift_cast(w, layer, col_start, width):
    rows = w.shape[1]
    base, shift = col_start // LANES, col_start % LANES
    tr = _pick(rows, 1024, 8)
    piece = lambda d: pl.BlockSpec((None, tr, LANES), lambda i, j: (layer, i, base + 2 * j + d))
    return pl.pallas_call(
        functools.partial(_shift_cast_kernel, shift=shift),
        grid=(rows // tr, width // (2 * LANES)),
        in_specs=[piece(0), piece(1), piece(2)],
        out_specs=pl.BlockSpec((tr, 2 * LANES), lambda i, j: (i, j)),
        out_shape=jax.ShapeDtypeStruct((rows, width), bf16),
        compiler_params=_params(("parallel", "parallel")),
        name="shift_cast",
    )(w, w, w)


def _prepare_weights(norm1_g, w_in, qkv_conv_w, a_log, dt_bias, dn_norm_g, sg_norm_g, sg_w, sg_b,
                     w_out, norm2_g, w_up, ffn_conv_w, w_down):
    depth = w_in.shape[0]
    heads = a_log.shape[-1]
    dnw = qkv_conv_w.shape[-1] // 3
    sgw = sg_norm_g.shape[-1]
    ff = ffn_conv_w.shape[-1]
    ffp = _round_up(ff, 1024)
    pad_f = ffp - ff
    g0 = 4 * dnw
    g1 = g0 + 4 * heads
    assert 4 * heads <= LANES
    sg_heads, sg_chunk = sg_w.shape[1], sg_w.shape[2]
    dims = dict(heads=heads, dnw=dnw, sgw=sgw, ffp=ffp, sg_heads=sg_heads, sg_chunk=sg_chunk)
    lead = jnp.zeros((2 * heads,), f32)
    tail = jnp.zeros((LANES - 4 * heads,), f32)
    layers = []
    for l in range(depth):
        wl = w_in[l]
        neg_a = -jnp.exp(a_log[l].astype(f32)).reshape(2 * heads)
        dtb = dt_bias[l].astype(f32).reshape(2 * heads)
        layers.append(dict(
            norm1_g=norm1_g[l],
            w_qkvz=_cast_pad(w_in, l, 1, [(0, g0)]),
            w_sg=_shift_cast(w_in, l, g1, 2 * sgw),
            w_gate=jnp.pad(wl[:, g0:g1], ((0, 0), (0, LANES - 4 * heads))).astype(bf16),
            qkv_cw=qkv_conv_w[l].astype(f32),
            prm=jnp.stack([jnp.concatenate([lead, neg_a, tail]), jnp.concatenate([lead, dtb, tail])]),
            dn_norm_g=dn_norm_g[l].astype(f32).reshape(1, -1),
            sg_norm_g=sg_norm_g[l].astype(f32).reshape(1, -1),
            sg_w=sg_w[l].astype(bf16),
            sg_bias=jnp.broadcast_to(sg_b[l].astype(f32)[..., None], (sg_heads, sg_chunk, sgw // sg_heads)),
            w_out=_cast_pad(w_out, l, 1, [(0, w_out.shape[2])]),
            norm2_g=norm2_g[l],
            w_up=_cast_pad(w_up, l, 1, [(0, ff), (ff, ff)], pad_to=1024),
            ffn_cw=jnp.pad(ffn_conv_w[l].astype(f32), ((0, 0), (0, pad_f))),
            w_down=_cast_pad(w_down, l, 0, [(0, ff)], pad_to=1024),
        ))
    return layers, dims


def _trunk(x, layers, dims, final_norm_g):
    nb, seq, d = x.shape
    heads, dnw, sgw, ffp = dims["heads"], dims["dnw"], dims["sgw"], dims["ffp"]
    head_dim = dnw // heads
    xt = x.reshape(nb * seq, d)
    for w in layers:
        hn = _rmsnorm(xt, w["norm1_g"], bf16)
        gates = _matmul(hn, w["w_gate"], f32, name="gate_proj")
        tm = _pick(seq, 1024, BF16_SUBLANES)
        halo = _matmul(_tile_halo_rows(hn, tm, seq), w["w_qkvz"], f32, n_cols=3 * dnw, name="qkv_halo")
        cw = w["qkv_cw"]
        (q,) = _proj_conv(hn, w["w_qkvz"], halo, cw[:, :dnw], 0, dnw, tm, head_dim, True, head_dim ** -0.5, False)
        k, kt = _proj_conv(hn, w["w_qkvz"], halo, cw[:, dnw:2 * dnw], dnw, dnw, tm, head_dim, True, 1.0, True)
        (v,) = _proj_conv(hn, w["w_qkvz"], halo, cw[:, 2 * dnw:], 2 * dnw, dnw, tm, head_dim, False, 1.0, False)
        z = _matmul(hn, w["w_qkvz"], bf16, col_off=3 * dnw, name="z_proj")
        p_sg = _matmul(hn, w["w_sg"], bf16, name="sg_proj")
        gcol, grow = _gate_prep(gates, w["prm"], heads)
        o_f, o_b = _deltanet(q, k, v, kt, gcol, grow, seq, heads, head_dim)
        o = _mixer_out(o_f, o_b, z, w["dn_norm_g"], heads, p_sg, sgw,
                       w["sg_w"], w["sg_bias"], w["sg_norm_g"], dims["sg_heads"], dims["sg_chunk"])
        xt = _matmul(o, w["w_out"], f32, res=xt, name="out_proj")
        h2 = _rmsnorm(xt, w["norm2_g"], bf16)
        hid = _up_conv_gate(h2, w["w_up"], w["ffn_cw"], ffp, seq)
        xt = _matmul(hid, w["w_down"], f32, res=xt, tk=2816, name="down_proj")
    return _rmsnorm(xt, final_norm_g, f32).reshape(nb, seq, d)


def kernel(x_prompt, x_sample, norm1_g, w_in, qkv_conv_w, a_log, dt_bias, dn_norm_g, sg_norm_g, sg_w, sg_b, w_out, norm2_g, w_up, ffn_conv_w, w_down, final_norm_g):
    layers, dims = _prepare_weights(norm1_g, w_in, qkv_conv_w, a_log, dt_bias, dn_norm_g, sg_norm_g, sg_w, sg_b,
                                    w_out, norm2_g, w_up, ffn_conv_w, w_down)
    return (_trunk(x_prompt, layers, dims, final_norm_g), _trunk(x_sample, layers, dims, final_norm_g))
```

```python
import functools
import math

import jax
import jax.numpy as jnp
import numpy as np
from jax import lax
from jax.experimental import pallas as pl
from jax.experimental.pallas import tpu as pltpu

NORM_EPS = 1e-6
L2_EPS = 1e-6
LANES = 128
BF16_SUBLANES = 16
DN_CHUNK = 128
LEAF = 16
DN_HEAD_BLOCK = 16
VMEM_LIMIT = 56 * 1024 * 1024
CAST_BLOCK_ELEMS = 1024 * 1024

f32 = jnp.float32
bf16 = jnp.bfloat16


def _pick(n, target, mult):
    best = None
    d = mult
    while d <= min(n, target):
        if n % d == 0:
            best = d
        d += mult
    return best if best is not None else n


def _round_up(n, m):
    return (n + m - 1) // m * m


def _params(sem):
    return pltpu.CompilerParams(dimension_semantics=sem, vmem_limit_bytes=VMEM_LIMIT)


def _bdot(a, b):
    return jnp.dot(a.astype(bf16), b.astype(bf16), preferred_element_type=f32)


def _rmsnorm_kernel(x_ref, g_ref, o_ref):
    x = x_ref[...]
    ms = jnp.mean(x * x, axis=-1, keepdims=True)
    o_ref[...] = (x * lax.rsqrt(ms + NORM_EPS) * g_ref[...]).astype(o_ref.dtype)


def _rmsnorm(x, g, out_dtype):
    t, d = x.shape
    tm = _pick(t, 256, 8)
    return pl.pallas_call(
        _rmsnorm_kernel,
        grid=(t // tm,),
        in_specs=[pl.BlockSpec((tm, d), lambda i: (i, 0)), pl.BlockSpec((1, d), lambda i: (0, 0))],
        out_specs=pl.BlockSpec((tm, d), lambda i: (i, 0)),
        out_shape=jax.ShapeDtypeStruct((t, d), out_dtype),
        compiler_params=_params(("parallel",)),
        name="rmsnorm",
    )(x, g.reshape(1, d).astype(f32))


def _mm_kernel(*refs, nk, has_res):
    if has_res:
        a_ref, w_ref, r_ref, o_ref = refs[:4]
        scratch = refs[4:]
    else:
        a_ref, w_ref, o_ref = refs[:3]
        r_ref = None
        scratch = refs[3:]
    part = jnp.dot(a_ref[...], w_ref[...], preferred_element_type=f32)
    if nk == 1:
        if r_ref is not None:
            part = part + r_ref[...]
        o_ref[...] = part.astype(o_ref.dtype)
        return
    acc_ref = scratch[0]
    k = pl.program_id(2)

    @pl.when(k == 0)
    def _():
        acc_ref[...] = part

    @pl.when(k > 0)
    def _():
        acc_ref[...] += part

    @pl.when(k == nk - 1)
    def _():
        out = acc_ref[...]
        if r_ref is not None:
            out = out + r_ref[...]
        o_ref[...] = out.astype(o_ref.dtype)


def _matmul(a, w, out_dtype, res=None, tm=1024, tn=1024, tk=4096, col_off=0, n_cols=None, name="matmul"):
    m, k = a.shape
    n = w.shape[1] - col_off if n_cols is None else n_cols
    tm = _pick(m, tm, 8)
    tn = _pick(math.gcd(n, col_off) if col_off else n, tn, LANES)
    tk = _pick(k, tk, LANES)
    nk = k // tk
    off = col_off // tn
    in_specs = [pl.BlockSpec((tm, tk), lambda i, j, kk: (i, kk)),
                pl.BlockSpec((tk, tn), lambda i, j, kk: (kk, j + off))]
    args = [a, w]
    if res is not None:
        in_specs.append(pl.BlockSpec((tm, tn), lambda i, j, kk: (i, j)))
        args.append(res)
    scratch = [pltpu.VMEM((tm, tn), f32)] if nk > 1 else []
    return pl.pallas_call(
        functools.partial(_mm_kernel, nk=nk, has_res=res is not None),
        grid=(m // tm, n // tn, nk),
        in_specs=in_specs,
        out_specs=pl.BlockSpec((tm, tn), lambda i, j, kk: (i, j)),
        out_shape=jax.ShapeDtypeStruct((m, n), out_dtype),
        scratch_shapes=scratch,
        compiler_params=_params(("parallel", "parallel", "arbitrary")),
        name=name,
    )(*args)


def _shifted(x, prev_row, next_row):
    ts = x.shape[0]
    rows = lax.broadcasted_iota(jnp.int32, x.shape, 0)
    xp = jnp.where(rows == 0, prev_row, pltpu.roll(x, 1, axis=0))
    xn = jnp.where(rows == ts - 1, next_row, pltpu.roll(x, ts - 1, axis=0))
    return xp, xn


def _tile_halo_rows(h, tm, seq):
    t, k = h.shape
    nt = t // tm
    h3 = h.reshape(nt, tm, k)
    zero = jnp.zeros((1, k), h.dtype)
    prev = jnp.concatenate([zero, h3[:-1, tm - 1]], axis=0)
    nxt = jnp.concatenate([h3[1:, 0], zero], axis=0)
    pos = jnp.arange(nt) % (seq // tm)
    prev = jnp.where((pos == 0)[:, None], jnp.zeros_like(prev), prev)
    nxt = jnp.where((pos == seq // tm - 1)[:, None], jnp.zeros_like(nxt), nxt)
    pad = jnp.zeros((nt, 6, k), h.dtype)
    return jnp.concatenate([prev[:, None], nxt[:, None], pad], axis=1).reshape(nt * 8, k)


def _proj_conv_kernel(x_ref, w_ref, halo_ref, cw_ref, *out_refs, nsplit, normalize, scale, head_dim):
    x = x_ref[...]
    o_ref = out_refs[0]
    w = o_ref.shape[1] // nsplit
    for s in range(nsplit):
        cs = slice(s * w, (s + 1) * w)
        pre = jnp.dot(x, w_ref[:, cs], preferred_element_type=f32)
        pp, pn = _shifted(pre, halo_ref[0:1, cs], halo_ref[1:2, cs])
        y = pp * cw_ref[0:1, cs] + pre * cw_ref[1:2, cs] + pn * cw_ref[2:3, cs]
        y = y * jax.nn.sigmoid(y)
        for hh in range(w // head_dim):
            seg = y[:, hh * head_dim:(hh + 1) * head_dim]
            if normalize:
                ss = jnp.sum(seg * seg, axis=-1, keepdims=True)
                seg = seg * lax.rsqrt(ss + L2_EPS)
                if scale != 1.0:
                    seg = seg * scale
            hs = slice(s * w + hh * head_dim, s * w + (hh + 1) * head_dim)
            o_ref[:, hs] = seg.astype(o_ref.dtype)
            if len(out_refs) > 1:
                out_refs[1][hs, :] = seg.T.astype(out_refs[1].dtype)


def _proj_conv(h, w, halo, conv_w, col_off, width, tm, head_dim, normalize, scale, transpose_out, tn=1024):
    t, k = h.shape
    tn = _pick(math.gcd(width, col_off) if col_off else width, tn, LANES)
    off = col_off // tn
    out_shape = [jax.ShapeDtypeStruct((t, width), bf16)]
    out_specs = [pl.BlockSpec((tm, tn), lambda i, j: (i, j))]
    if transpose_out:
        out_shape.append(jax.ShapeDtypeStruct((width, t), bf16))
        out_specs.append(pl.BlockSpec((tn, tm), lambda i, j: (j, i)))
    return pl.pallas_call(
        functools.partial(_proj_conv_kernel, nsplit=max(1, tn // 256), normalize=normalize, scale=scale,
                          head_dim=head_dim),
        grid=(t // tm, width // tn),
        in_specs=[pl.BlockSpec((tm, k), lambda i, j: (i, 0)),
                  pl.BlockSpec((k, tn), lambda i, j: (0, j + off)),
                  pl.BlockSpec((8, tn), lambda i, j: (i, j + off)),
                  pl.BlockSpec((3, tn), lambda i, j: (0, j))],
        out_specs=out_specs,
        out_shape=out_shape,
        compiler_params=_params(("parallel", "arbitrary")),
        name="proj_conv",
    )(h, w, halo, conv_w)


def _gate_prep_kernel(gate_ref, prm_ref, gcol_ref, grow_ref, *, heads):
    c_len = DN_CHUNK
    row = lax.broadcasted_iota(jnp.int32, (c_len, c_len), 0)
    col = lax.broadcasted_iota(jnp.int32, (c_len, c_len), 1)
    tri_lo = (row >= col).astype(f32)
    tri_up = (row <= col).astype(f32)
    for c in range(gate_ref.shape[0] // c_len):
        rows = slice(c * c_len, (c + 1) * c_len)
        blk = gate_ref[rows, :]
        beta = jax.nn.sigmoid(blk)
        g = prm_ref[0:1, :] * jax.nn.softplus(blk + prm_ref[1:2, :])
        g_f = jnp.dot(tri_lo, g, preferred_element_type=f32, precision=lax.Precision.HIGHEST)
        g_b = jnp.dot(tri_up, g, preferred_element_type=f32, precision=lax.Precision.HIGHEST)
        out = jnp.where(col < 2 * heads, beta, jnp.where(col < 3 * heads, g_f, jnp.where(col < 4 * heads, g_b, 0.0)))
        gcol_ref[rows, :] = out
        grow_ref[c] = out.T


def _gate_prep(gates, prm, heads):
    t = gates.shape[0]
    n = t // DN_CHUNK
    cb = _pick(n, 8, 1)
    return pl.pallas_call(
        functools.partial(_gate_prep_kernel, heads=heads),
        grid=(n // cb,),
        in_specs=[pl.BlockSpec((cb * DN_CHUNK, LANES), lambda i: (i, 0)), pl.BlockSpec((2, LANES), lambda i: (0, 0))],
        out_specs=[pl.BlockSpec((cb * DN_CHUNK, LANES), lambda i: (i, 0)),
                   pl.BlockSpec((cb, LANES, DN_CHUNK), lambda i: (i, 0, 0))],
        out_shape=[jax.ShapeDtypeStruct((t, LANES), f32), jax.ShapeDtypeStruct((n, LANES, DN_CHUNK), f32)],
        compiler_params=_params(("parallel",)),
        name="gate_prep",
    )(gates, prm)


_M_NEG_LO, _M_NEG_UP, _M_OFFDIAG, _M_EYE, _M_LEAF, _M_OFF0 = 0, 1, 2, 3, 4, 5


def _dn_masks():
    c = DN_CHUNK
    r = np.arange(c)[:, None]
    s = np.arange(c)[None, :]
    ms = [np.where(r >= s, 0.0, -np.inf), np.where(r <= s, 0.0, -np.inf), (r != s) * 1.0, (r == s) * 1.0,
          (r // LEAF == s // LEAF) * 1.0]
    blk = LEAF
    while blk < c:
        ms.append(((r // (2 * blk) == s // (2 * blk)) & (r // blk != s // blk)) * 1.0)
        blk *= 2
    return jnp.asarray(np.stack(ms), f32)


def _block_rows(x, blk, parity):
    return jnp.concatenate([x[i * blk:(i + 1) * blk] for i in range(x.shape[0] // blk) if i % 2 == parity], axis=0)


def _merge_block_rows(base, rows, blk, parity, combine):
    out = []
    for i in range(base.shape[0] // blk):
        piece = base[i * blk:(i + 1) * blk]
        if i % 2 == parity:
            piece = combine(piece, rows[(i // 2) * blk:(i // 2 + 1) * blk])
        out.append(piece)
    return jnp.concatenate(out, axis=0)


def _tri_inverse(a_mats, lower, m_ref):
    n = a_mats[0].shape[1]
    leaf, eye = m_ref[_M_LEAF], m_ref[_M_EYE]
    ps = [-(a * leaf) for a in a_mats]
    xs = [eye + p for p in ps]
    ps = [_bdot(p, p) for p in ps]
    for _ in range(int(math.log2(LEAF)) - 2):
        pbs = [p.astype(bf16) for p in ps]
        rs = [jnp.dot(pb, jnp.concatenate([pb, x.astype(bf16)], axis=1), preferred_element_type=f32)
              for pb, x in zip(pbs, xs)]
        ps = [r[:, :n] for r in rs]
        xs = [x + r[:, n:] for x, r in zip(xs, rs)]
    ts = [x + _bdot(p, x) for p, x in zip(ps, xs)]
    parity = [1 if lo else 0 for lo in lower]
    blk = LEAF
    for lvl in range(m_ref.shape[0] - _M_OFF0):
        off = m_ref[_M_OFF0 + lvl]
        zeros = jnp.zeros((n, n), bf16)
        vs = [_bdot(_block_rows(a * off, blk, pr), t).astype(bf16) for a, t, pr in zip(a_mats, ts, parity)]
        v_full = [_merge_block_rows(zeros, v, blk, pr, lambda _, r: r) for v, pr in zip(vs, parity)]
        ds = [_bdot(_block_rows(t, blk, pr), vf) for t, vf, pr in zip(ts, v_full, parity)]
        ts = [_merge_block_rows(t, d, blk, pr, lambda p, r: p - r) for t, d, pr in zip(ts, ds, parity)]
        blk *= 2
    return ts


def _dn_kernel(qf_ref, kf_ref, vf_ref, ktf_ref, gcf_ref, grf_ref,
               qb_ref, kb_ref, vb_ref, ktb_ref, gcb_ref, grb_ref, m_ref,
               of_ref, ob_ref, st_s, *, heads, head_block, head_dim):
    c_len = DN_CHUNK
    hb = pl.program_id(1)

    @pl.when(pl.program_id(2) == 0)
    def _():
        st_s[...] = jnp.zeros_like(st_s)

    lane = lax.broadcasted_iota(jnp.int32, (c_len, LANES), 1)
    dirs = ((qf_ref, kf_ref, vf_ref, ktf_ref, gcf_ref, grf_ref, of_ref, _M_NEG_LO, c_len - 1),
            (qb_ref, kb_ref, vb_ref, ktb_ref, gcb_ref, grb_ref, ob_ref, _M_NEG_UP, 0))
    chains = [(d, hh) for d in range(2) for hh in range(head_block)]
    sls = [slice(hh * head_dim, (hh + 1) * head_dim) for _, hh in chains]

    q_c = [dirs[d][0][:, sl] for (d, _), sl in zip(chains, sls)]
    k_c = [dirs[d][1][:, sl] for (d, _), sl in zip(chains, sls)]
    kt_c = [dirs[d][3][sl, :] for (d, _), sl in zip(chains, sls)]
    states = [st_s[d, hh] for d, hh in chains]
    gram = [jnp.dot(jnp.concatenate([q, k], axis=0), kt, preferred_element_type=f32)
            for q, k, kt in zip(q_c, k_c, kt_c)]
    kqs = [jnp.dot(jnp.concatenate([k, q], axis=0), s.astype(bf16), preferred_element_type=f32)
           for q, k, s in zip(q_c, k_c, states)]

    gc, gr, br = [], [], []
    for d, hh in chains:
        head = hb * head_block + hh
        g_col = (2 + d) * heads + head
        gc.append(jnp.sum(jnp.where(lane == g_col, dirs[d][4][...], 0.0), axis=-1, keepdims=True))
        gr.append(dirs[d][5][0, pl.ds(g_col, 1), :])
        br.append(dirs[d][5][0, pl.ds(d * heads + head, 1), :])

    e = [jnp.exp(c - r + m_ref[dirs[d][7]]) * b for (d, _), c, r, b in zip(chains, gc, gr, br)]
    attn = [(g[:c_len] * ee).astype(bf16) for g, ee in zip(gram, e)]
    offdiag = m_ref[_M_OFFDIAG]
    tinv = _tri_inverse([g[c_len:] * ee * offdiag for g, ee in zip(gram, e)], [d == 0 for d, _ in chains], m_ref)

    eg = [jnp.exp(c) for c in gc]
    r = [dirs[d][2][:, sl].astype(f32) - ee * ks[:c_len] for (d, _), sl, ee, ks in zip(chains, sls, eg, kqs)]
    y = [jnp.dot(t.astype(bf16), rr.astype(bf16), preferred_element_type=f32).astype(bf16)
         for t, rr in zip(tinv, r)]
    g_end = [jnp.sum(jnp.where(lane[0:1, :] == dirs[d][8], r_, 0.0), axis=-1, keepdims=True)
             for (d, _), r_ in zip(chains, gr)]
    kt_dec = [(kt.astype(f32) * (jnp.exp(ge - r_) * b)).astype(bf16)
              for kt, ge, r_, b in zip(kt_c, g_end, gr, br)]
    ay = [jnp.dot(jnp.concatenate([a, kd], axis=0), yy, preferred_element_type=f32)
          for a, kd, yy in zip(attn, kt_dec, y)]
    for (d, hh), sl, ee, ks, ay_, ge, s in zip(chains, sls, eg, kqs, ay, g_end, states):
        dirs[d][6][:, sl] = (ee * ks[c_len:] + ay_[:c_len]).astype(dirs[d][6].dtype)
        st_s[d, hh] = jnp.exp(ge) * s + ay_[c_len:]


def _deltanet(q, k, v, kt, gcol, grow, seq, heads, head_dim):
    t = q.shape[0]
    nb = t // seq
    n = seq // DN_CHUNK
    hblk = _pick(heads, DN_HEAD_BLOCK, 1)
    w = hblk * head_dim
    fwd = lambda b, h, j: b * n + j
    bwd = lambda b, h, j: b * n + (n - 1 - j)

    def specs(pos):
        tok = pl.BlockSpec((DN_CHUNK, w), lambda b, h, j: (pos(b, h, j), h))
        return [tok, tok, tok,
                pl.BlockSpec((w, DN_CHUNK), lambda b, h, j: (h, pos(b, h, j))),
                pl.BlockSpec((DN_CHUNK, LANES), lambda b, h, j: (pos(b, h, j), 0)),
                pl.BlockSpec((1, LANES, DN_CHUNK), lambda b, h, j: (pos(b, h, j), 0, 0))]

    masks = _dn_masks()
    out_sds = jax.ShapeDtypeStruct((t, heads * head_dim), bf16)
    return pl.pallas_call(
        functools.partial(_dn_kernel, heads=heads, head_block=hblk, head_dim=head_dim),
        grid=(nb, heads // hblk, n),
        in_specs=specs(fwd) + specs(bwd) + [pl.BlockSpec(masks.shape, lambda b, h, j: (0, 0, 0))],
        out_specs=[pl.BlockSpec((DN_CHUNK, w), lambda b, h, j: (fwd(b, h, j), h)),
                   pl.BlockSpec((DN_CHUNK, w), lambda b, h, j: (bwd(b, h, j), h))],
        out_shape=[out_sds, out_sds],
        scratch_shapes=[pltpu.VMEM((2, hblk, head_dim, head_dim), f32)],
        compiler_params=_params(("parallel", "parallel", "arbitrary")),
        name="deltanet",
    )(q, k, v, kt, gcol, grow, q, k, v, kt, gcol, grow, masks)


def _mix_kernel(of_ref, ob_ref, z_ref, dng_ref, u_ref, v_ref, ws_ref, bias_ref, sgg_ref, o_ref, *,
                dn_heads, dn_dim, sg_heads, sg_dim, sg_chunk):
    dnw = dn_heads * dn_dim
    for hh in range(dn_heads):
        sl = slice(hh * dn_dim, (hh + 1) * dn_dim)
        o = of_ref[:, sl].astype(f32) + ob_ref[:, sl].astype(f32)
        ms = jnp.mean(o * o, axis=-1, keepdims=True)
        z = z_ref[:, sl].astype(f32)
        o_ref[:, sl] = (o * lax.rsqrt(ms + NORM_EPS) * dng_ref[...] * (z * jax.nn.sigmoid(z))).astype(o_ref.dtype)
    for c in range(o_ref.shape[0] // sg_chunk):
        rows = slice(c * sg_chunk, (c + 1) * sg_chunk)
        for g in range(sg_heads):
            sl = slice(g * sg_dim, (g + 1) * sg_dim)
            v = jax.nn.gelu(v_ref[rows, sl].astype(f32))
            ms = jnp.mean(v * v, axis=-1, keepdims=True)
            vn = v * lax.rsqrt(ms + NORM_EPS) * sgg_ref[:, sl]
            mixed = jnp.dot(ws_ref[g], vn.astype(bf16), preferred_element_type=f32) + bias_ref[g]
            u = jax.nn.gelu(u_ref[rows, sl].astype(f32))
            o_ref[rows, dnw + g * sg_dim:dnw + (g + 1) * sg_dim] = (u * mixed).astype(o_ref.dtype)


def _mixer_out(o_f, o_b, z, dn_norm_g, dn_heads, p_sg, sgw, ws, bias, sg_norm_g, sg_heads, sg_chunk):
    t, dnw = o_f.shape
    tb = _pick(t, 2 * sg_chunk, sg_chunk)
    sg_dim = sgw // sg_heads
    return pl.pallas_call(
        functools.partial(_mix_kernel, dn_heads=dn_heads, dn_dim=dnw // dn_heads, sg_heads=sg_heads,
                          sg_dim=sg_dim, sg_chunk=sg_chunk),
        grid=(t // tb,),
        in_specs=[pl.BlockSpec((tb, dnw), lambda i: (i, 0)), pl.BlockSpec((tb, dnw), lambda i: (i, 0)),
                  pl.BlockSpec((tb, dnw), lambda i: (i, 0)),
                  pl.BlockSpec((1, dnw // dn_heads), lambda i: (0, 0)),
                  pl.BlockSpec((tb, sgw), lambda i: (i, 0)), pl.BlockSpec((tb, sgw), lambda i: (i, 1)),
                  pl.BlockSpec((sg_heads, sg_chunk, sg_chunk), lambda i: (0, 0, 0)),
                  pl.BlockSpec((sg_heads, sg_chunk, sg_dim), lambda i: (0, 0, 0)),
                  pl.BlockSpec((1, sgw), lambda i: (0, 0))],
        out_specs=pl.BlockSpec((tb, dnw + sgw), lambda i: (i, 0)),
        out_shape=jax.ShapeDtypeStruct((t, dnw + sgw), bf16),
        compiler_params=_params(("parallel",)),
        name="mixer_out",
    )(o_f, o_b, z, dn_norm_g, p_sg, p_sg, ws, bias, sg_norm_g)


def _up_kernel(x_ref, wg_ref, wu_ref, halo_ref, cw_ref, o_ref, *, nsplit):
    x = x_ref[...]
    w = o_ref.shape[1] // nsplit
    for s in range(nsplit):
        cs = slice(s * w, (s + 1) * w)
        g = jnp.dot(x, wg_ref[:, cs], preferred_element_type=f32)
        u = jnp.dot(x, wu_ref[:, cs], preferred_element_type=f32)
        gp, gn = _shifted(g, halo_ref[0:1, cs], halo_ref[1:2, cs])
        y = gp * cw_ref[0:1, cs] + g * cw_ref[1:2, cs] + gn * cw_ref[2:3, cs]
        o_ref[:, cs] = (y * jax.nn.sigmoid(y) * u).astype(o_ref.dtype)


def _up_conv_gate(h, w_up, conv_w, ffp, seq, tm=1024, tn=1024):
    t, k = h.shape
    tm = _pick(seq, tm, BF16_SUBLANES)
    tn = _pick(ffp, tn, LANES)
    halo = _matmul(_tile_halo_rows(h, tm, seq), w_up, f32, n_cols=ffp, name="up_halo")
    uoff = ffp // tn
    return pl.pallas_call(
        functools.partial(_up_kernel, nsplit=max(1, tn // 256)),
        grid=(t // tm, ffp // tn),
        in_specs=[pl.BlockSpec((tm, k), lambda i, j: (i, 0), pipeline_mode=pl.Buffered(1)),
                  pl.BlockSpec((k, tn), lambda i, j: (0, j)),
                  pl.BlockSpec((k, tn), lambda i, j: (0, j + uoff)),
                  pl.BlockSpec((8, tn), lambda i, j: (i, j)),
                  pl.BlockSpec((3, tn), lambda i, j: (0, j))],
        out_specs=pl.BlockSpec((tm, tn), lambda i, j: (i, j)),
        out_shape=jax.ShapeDtypeStruct((t, ffp), bf16),
        compiler_params=_params(("parallel", "arbitrary")),
        name="up_conv_gate",
    )(h, w_up, w_up, halo, conv_w)


def _cast_kernel(x_ref, o_ref, *, zero_blocks, axis):
    blk = pl.program_id(axis)
    is_zero = functools.reduce(jnp.logical_or, [blk == z for z in zero_blocks], False)
    o_ref[...] = jnp.where(is_zero, 0.0, x_ref[...]).astype(o_ref.dtype)


def _cast_pad(w, layer, axis, segments, pad_to=None):
    _, rows, cols = w.shape
    blk = functools.reduce(math.gcd, [v for seg in segments for v in seg if v] + [1024])
    src_of, zero_blocks = [], []
    for start, length in segments:
        padded = _round_up(length, pad_to) if pad_to else length
        for b in range(padded // blk):
            if b * blk < length:
                src_of.append((start + b * blk) // blk)
            else:
                zero_blocks.append(len(src_of))
                src_of.append(src_of[-1])
    nblk = len(src_of)
    table = jnp.asarray(src_of, jnp.int32)
    if axis == 1:
        tr = _pick(rows, CAST_BLOCK_ELEMS // blk, 8)
        grid = (rows // tr, nblk)
        in_spec = pl.BlockSpec((None, tr, blk), lambda i, j, tab: (layer, i, tab[j]))
        out_spec = pl.BlockSpec((tr, blk), lambda i, j, tab: (i, j))
        out_shape = (rows, nblk * blk)
    else:
        grid = (1, nblk)
        in_spec = pl.BlockSpec((None, blk, cols), lambda i, j, tab: (layer, tab[j], 0))
        out_spec = pl.BlockSpec((blk, cols), lambda i, j, tab: (j, 0))
        out_shape = (nblk * blk, cols)

    def body(tab_ref, x_ref, o_ref):
        _cast_kernel(x_ref, o_ref, zero_blocks=zero_blocks, axis=1)

    return pl.pallas_call(
        body,
        grid_spec=pltpu.PrefetchScalarGridSpec(num_scalar_prefetch=1, grid=grid, in_specs=[in_spec],
                                               out_specs=out_spec),
        out_shape=jax.ShapeDtypeStruct(out_shape, bf16),
        compiler_params=_params(("parallel", "parallel")),
        name="cast_pad",
    )(table, w)


def _cast_transpose_kernel(a_ref, b_ref, o_ref):
    o_ref[...] = jnp.concatenate([a_ref[...], b_ref[...]], axis=0).T.astype(o_ref.dtype)


def _cast_transpose(w_t, layer, row_start, n_rows):
    half = LANES // 2
    assert row_start % half == 0 and n_rows % LANES == 0
    k = w_t.shape[2]
    base = row_start // half
    piece = lambda d: pl.BlockSpec((None, half, k), lambda j: (layer, base + 2 * j + d, 0))
    return pl.pallas_call(
        _cast_transpose_kernel,
        grid=(n_rows // LANES,),
        in_specs=[piece(0), piece(1)],
        out_specs=pl.BlockSpec((k, LANES), lambda j: (0, j)),
        out_shape=jax.ShapeDtypeStruct((k, n_rows), bf16),
        compiler_params=_params(("parallel",)),
        name="cast_transpose",
    )(w_t, w_t)


def _prepare_weights(norm1_g, w_in, qkv_conv_w, a_log, dt_bias, dn_norm_g, sg_norm_g, sg_w, sg_b,
                     w_out, norm2_g, w_up, ffn_conv_w, w_down):
    depth = w_in.shape[0]
    heads = a_log.shape[-1]
    dnw = qkv_conv_w.shape[-1] // 3
    sgw = sg_norm_g.shape[-1]
    ff = ffn_conv_w.shape[-1]
    ffp = _round_up(ff, 1024)
    pad_f = ffp - ff
    g0 = 4 * dnw
    g1 = g0 + 4 * heads
    assert 4 * heads <= LANES
    sg_heads, sg_chunk = sg_w.shape[1], sg_w.shape[2]
    dims = dict(heads=heads, dnw=dnw, sgw=sgw, ffp=ffp, sg_heads=sg_heads, sg_chunk=sg_chunk)
    lead = jnp.zeros((2 * heads,), f32)
    tail = jnp.zeros((LANES - 4 * heads,), f32)
    layers = []
    w_in_t = jnp.swapaxes(w_in, 1, 2)
    for l in range(depth):
        neg_a = -jnp.exp(a_log[l].astype(f32)).reshape(2 * heads)
        dtb = dt_bias[l].astype(f32).reshape(2 * heads)
        layers.append(dict(
            norm1_g=norm1_g[l],
            w_qkvz=_cast_transpose(w_in_t, l, 0, g0),
            w_sg=_cast_transpose(w_in_t, l, g1, 2 * sgw),
            w_gate=_cast_transpose(w_in_t, l, g0, LANES),
            qkv_cw=qkv_conv_w[l].astype(f32),
            prm=jnp.stack([jnp.concatenate([lead, neg_a, tail]), jnp.concatenate([lead, dtb, tail])]),
            dn_norm_g=dn_norm_g[l].astype(f32).reshape(1, -1),
            sg_norm_g=sg_norm_g[l].astype(f32).reshape(1, -1),
            sg_w=sg_w[l].astype(bf16),
            sg_bias=jnp.broadcast_to(sg_b[l].astype(f32)[..., None], (sg_heads, sg_chunk, sgw // sg_heads)),
            w_out=_cast_pad(w_out, l, 1, [(0, w_out.shape[2])]),
            norm2_g=norm2_g[l],
            w_up=_cast_pad(w_up, l, 1, [(0, ff), (ff, ff)], pad_to=1024),
            ffn_cw=jnp.pad(ffn_conv_w[l].astype(f32), ((0, 0), (0, pad_f))),
            w_down=_cast_pad(w_down, l, 0, [(0, ff)], pad_to=1024),
        ))
    return layers, dims


def _trunk(x, layers, dims, final_norm_g):
    nb, seq, d = x.shape
    heads, dnw, sgw, ffp = dims["heads"], dims["dnw"], dims["sgw"], dims["ffp"]
    head_dim = dnw // heads
    xt = x.reshape(nb * seq, d)
    for w in layers:
        hn = _rmsnorm(xt, w["norm1_g"], bf16)
        gates = _matmul(hn, w["w_gate"], f32, name="gate_proj")
        tm = _pick(seq, 1024, BF16_SUBLANES)
        halo = _matmul(_tile_halo_rows(hn, tm, seq), w["w_qkvz"], f32, n_cols=3 * dnw, name="qkv_halo")
        cw = w["qkv_cw"]
        (q,) = _proj_conv(hn, w["w_qkvz"], halo, cw[:, :dnw], 0, dnw, tm, head_dim, True, head_dim ** -0.5, False)
        k, kt = _proj_conv(hn, w["w_qkvz"], halo, cw[:, dnw:2 * dnw], dnw, dnw, tm, head_dim, True, 1.0, True)
        (v,) = _proj_conv(hn, w["w_qkvz"], halo, cw[:, 2 * dnw:], 2 * dnw, dnw, tm, head_dim, False, 1.0, False)
        z = _matmul(hn, w["w_qkvz"], bf16, col_off=3 * dnw, name="z_proj")
        p_sg = _matmul(hn, w["w_sg"], bf16, name="sg_proj")
        gcol, grow = _gate_prep(gates, w["prm"], heads)
        o_f, o_b = _deltanet(q, k, v, kt, gcol, grow, seq, heads, head_dim)
        o = _mixer_out(o_f, o_b, z, w["dn_norm_g"], heads, p_sg, sgw,
                       w["sg_w"], w["sg_bias"], w["sg_norm_g"], dims["sg_heads"], dims["sg_chunk"])
        xt = _matmul(o, w["w_out"], f32, res=xt, name="out_proj")
        h2 = _rmsnorm(xt, w["norm2_g"], bf16)
        hid = _up_conv_gate(h2, w["w_up"], w["ffn_cw"], ffp, seq)
        xt = _matmul(hid, w["w_down"], f32, res=xt, tk=2816, name="down_proj")
    return _rmsnorm(xt, final_norm_g, f32).reshape(nb, seq, d)


def kernel(x_prompt, x_sample, norm1_g, w_in, qkv_conv_w, a_log, dt_bias, dn_norm_g, sg_norm_g, sg_w, sg_b, w_out, norm2_g, w_up, ffn_conv_w, w_down, final_norm_g):
    layers, dims = _prepare_weights(norm1_g, w_in, qkv_conv_w, a_log, dt_bias, dn_norm_g, sg_norm_g, sg_w, sg_b,
                                    w_out, norm2_g, w_up, ffn_conv_w, w_down)
    return (_trunk(x_prompt, layers, dims, final_norm_g), _trunk(x_sample, layers, dims, final_norm_g))
```

```python
import functools
import math

import jax
import jax.numpy as jnp
import numpy as np
from jax import lax
from jax.experimental import pallas as pl
from jax.experimental.pallas import tpu as pltpu

NORM_EPS = 1e-6
L2_EPS = 1e-6
LANES = 128
BF16_SUBLANES = 16
DN_CHUNK = 128
LEAF = 16
DN_HEAD_BLOCK = 16
VMEM_LIMIT = 56 * 1024 * 1024
CAST_BLOCK_ELEMS = 1024 * 1024

f32 = jnp.float32
bf16 = jnp.bfloat16


def _pick(n, target, mult):
    best = None
    d = mult
    while d <= min(n, target):
        if n % d == 0:
            best = d
        d += mult
    return best if best is not None else n


def _round_up(n, m):
    return (n + m - 1) // m * m


def _params(sem):
    return pltpu.CompilerParams(dimension_semantics=sem, vmem_limit_bytes=VMEM_LIMIT)


def _bdot(a, b):
    return jnp.dot(a.astype(bf16), b.astype(bf16), preferred_element_type=f32)


def _rmsnorm_kernel(x_ref, g_ref, o_ref):
    x = x_ref[...]
    ms = jnp.mean(x * x, axis=-1, keepdims=True)
    o_ref[...] = (x * lax.rsqrt(ms + NORM_EPS) * g_ref[...]).astype(o_ref.dtype)


def _rmsnorm(x, g, out_dtype):
    t, d = x.shape
    tm = _pick(t, 256, 8)
    return pl.pallas_call(
        _rmsnorm_kernel,
        grid=(t // tm,),
        in_specs=[pl.BlockSpec((tm, d), lambda i: (i, 0)), pl.BlockSpec((1, d), lambda i: (0, 0))],
        out_specs=pl.BlockSpec((tm, d), lambda i: (i, 0)),
        out_shape=jax.ShapeDtypeStruct((t, d), out_dtype),
        compiler_params=_params(("parallel",)),
        name="rmsnorm",
    )(x, g.reshape(1, d).astype(f32))


def _rowstats_kernel(x_ref, xb_ref, ss_ref):
    x = x_ref[...]
    xb_ref[...] = x.astype(xb_ref.dtype)
    ss_ref[...] = jnp.broadcast_to(jnp.sum(x * x, axis=-1, keepdims=True), ss_ref.shape)


def _rowstats(x):
    t, d = x.shape
    tm = _pick(t, 256, 8)
    return pl.pallas_call(
        _rowstats_kernel,
        grid=(t // tm,),
        in_specs=[pl.BlockSpec((tm, d), lambda i: (i, 0))],
        out_specs=[pl.BlockSpec((tm, d), lambda i: (i, 0)), pl.BlockSpec((tm, LANES), lambda i: (i, 0))],
        out_shape=[jax.ShapeDtypeStruct((t, d), bf16), jax.ShapeDtypeStruct((t, LANES), f32)],
        compiler_params=_params(("parallel",)),
        name="rowstats",
    )(x)


def _row_scale(ss_ref, d, width):
    r = lax.rsqrt(ss_ref[...] * (1.0 / d) + NORM_EPS)
    return r if width == LANES else jnp.concatenate([r] * (width // LANES), axis=1)


def _mm_kernel(*refs, nk, has_res, has_scale, emit_stats, d_norm):
    refs = list(refs)
    a_ref, w_ref = refs[:2]
    pos = 2
    r_ref = s_ref = None
    if has_res:
        r_ref = refs[pos]
        pos += 1
    if has_scale:
        s_ref = refs[pos]
        pos += 1
    o_ref = refs[pos]
    pos += 1
    xb_ref = ss_ref = None
    if emit_stats:
        xb_ref, ss_ref = refs[pos:pos + 2]
        pos += 2
    scratch = refs[pos:]

    def finish(out):
        if s_ref is not None:
            out = out * _row_scale(s_ref, d_norm, out.shape[1])
        if r_ref is not None:
            out = out + r_ref[...]
        o_ref[...] = out.astype(o_ref.dtype)
        if emit_stats:
            xb_ref[...] = out.astype(xb_ref.dtype)
            part = jnp.broadcast_to(jnp.sum(out * out, axis=-1, keepdims=True), ss_ref.shape)
            j = pl.program_id(1)

            @pl.when(j == 0)
            def _():
                ss_ref[...] = part

            @pl.when(j > 0)
            def _():
                ss_ref[...] += part

    part = jnp.dot(a_ref[...], w_ref[...], preferred_element_type=f32)
    if nk == 1:
        finish(part)
        return
    acc_ref = scratch[0]
    k = pl.program_id(2)

    @pl.when(k == 0)
    def _():
        acc_ref[...] = part

    @pl.when(k > 0)
    def _():
        acc_ref[...] += part

    @pl.when(k == nk - 1)
    def _():
        finish(acc_ref[...])


def _matmul(a, w, out_dtype, res=None, scale=None, emit_stats=False, tm=1024, tn=1024, tk=4096, col_off=0,
            n_cols=None, name="matmul"):
    m, k = a.shape
    n = w.shape[1] - col_off if n_cols is None else n_cols
    tm = _pick(m, tm, 8)
    tn = _pick(math.gcd(n, col_off) if col_off else n, tn, LANES)
    tk = _pick(k, tk, LANES)
    nk = k // tk
    off = col_off // tn
    in_specs = [pl.BlockSpec((tm, tk), lambda i, j, kk: (i, kk)),
                pl.BlockSpec((tk, tn), lambda i, j, kk: (kk, j + off))]
    args = [a, w]
    if res is not None:
        in_specs.append(pl.BlockSpec((tm, tn), lambda i, j, kk: (i, j)))
        args.append(res)
    if scale is not None:
        in_specs.append(pl.BlockSpec((tm, LANES), lambda i, j, kk: (i, 0)))
        args.append(scale)
    out_specs = [pl.BlockSpec((tm, tn), lambda i, j, kk: (i, j))]
    out_shape = [jax.ShapeDtypeStruct((m, n), out_dtype)]
    if emit_stats:
        out_specs += [pl.BlockSpec((tm, tn), lambda i, j, kk: (i, j)), pl.BlockSpec((tm, LANES), lambda i, j, kk: (i, 0))]
        out_shape += [jax.ShapeDtypeStruct((m, n), bf16), jax.ShapeDtypeStruct((m, LANES), f32)]
    scratch = [pltpu.VMEM((tm, tn), f32)] if nk > 1 else []
    outs = pl.pallas_call(
        functools.partial(_mm_kernel, nk=nk, has_res=res is not None, has_scale=scale is not None,
                          emit_stats=emit_stats, d_norm=k),
        grid=(m // tm, n // tn, nk),
        in_specs=in_specs,
        out_specs=out_specs,
        out_shape=out_shape,
        scratch_shapes=scratch,
        compiler_params=_params(("parallel", "arbitrary", "arbitrary")),
        name=name,
    )(*args)
    return outs if emit_stats else outs[0]


def _shifted(x, prev_row, next_row):
    ts = x.shape[0]
    rows = lax.broadcasted_iota(jnp.int32, x.shape, 0)
    xp = jnp.where(rows == 0, prev_row, pltpu.roll(x, 1, axis=0))
    xn = jnp.where(rows == ts - 1, next_row, pltpu.roll(x, ts - 1, axis=0))
    return xp, xn


def _tile_halo_rows(h, tm, seq):
    t, k = h.shape
    nt = t // tm
    h3 = h.reshape(nt, tm, k)
    zero = jnp.zeros((1, k), h.dtype)
    prev = jnp.concatenate([zero, h3[:-1, tm - 1]], axis=0)
    nxt = jnp.concatenate([h3[1:, 0], zero], axis=0)
    pos = jnp.arange(nt) % (seq // tm)
    prev = jnp.where((pos == 0)[:, None], jnp.zeros_like(prev), prev)
    nxt = jnp.where((pos == seq // tm - 1)[:, None], jnp.zeros_like(nxt), nxt)
    pad = jnp.zeros((nt, 6, k), h.dtype)
    return jnp.concatenate([prev[:, None], nxt[:, None], pad], axis=1).reshape(nt * 8, k)


def _proj_conv_kernel(x_ref, w_ref, halo_ref, cw_ref, ss_ref, *out_refs, nsplit, normalize, scale, head_dim):
    x = x_ref[...]
    o_ref = out_refs[0]
    w = o_ref.shape[1] // nsplit
    rs = _row_scale(ss_ref, x.shape[1], w)
    for s in range(nsplit):
        cs = slice(s * w, (s + 1) * w)
        pre = jnp.dot(x, w_ref[:, cs], preferred_element_type=f32) * rs
        pp, pn = _shifted(pre, halo_ref[0:1, cs], halo_ref[1:2, cs])
        y = pp * cw_ref[0:1, cs] + pre * cw_ref[1:2, cs] + pn * cw_ref[2:3, cs]
        y = y * jax.nn.sigmoid(y)
        for hh in range(w // head_dim):
            seg = y[:, hh * head_dim:(hh + 1) * head_dim]
            if normalize:
                ss = jnp.sum(seg * seg, axis=-1, keepdims=True)
                seg = seg * lax.rsqrt(ss + L2_EPS)
                if scale != 1.0:
                    seg = seg * scale
            hs = slice(s * w + hh * head_dim, s * w + (hh + 1) * head_dim)
            o_ref[:, hs] = seg.astype(o_ref.dtype)
            if len(out_refs) > 1:
                out_refs[1][hs, :] = seg.T.astype(out_refs[1].dtype)


def _proj_conv(h, ss, w, halo, conv_w, col_off, width, tm, head_dim, normalize, scale, transpose_out, tn=1024):
    t, k = h.shape
    tn = _pick(math.gcd(width, col_off) if col_off else width, tn, LANES)
    off = col_off // tn
    out_shape = [jax.ShapeDtypeStruct((t, width), bf16)]
    out_specs = [pl.BlockSpec((tm, tn), lambda i, j: (i, j))]
    if transpose_out:
        out_shape.append(jax.ShapeDtypeStruct((width, t), bf16))
        out_specs.append(pl.BlockSpec((tn, tm), lambda i, j: (j, i)))
    return pl.pallas_call(
        functools.partial(_proj_conv_kernel, nsplit=max(1, tn // 256), normalize=normalize, scale=scale,
                          head_dim=head_dim),
        grid=(t // tm, width // tn),
        in_specs=[pl.BlockSpec((tm, k), lambda i, j: (i, 0)),
                  pl.BlockSpec((k, tn), lambda i, j: (0, j + off)),
                  pl.BlockSpec((8, tn), lambda i, j: (i, j + off)),
                  pl.BlockSpec((3, tn), lambda i, j: (0, j)),
                  pl.BlockSpec((tm, LANES), lambda i, j: (i, 0))],
        out_specs=out_specs,
        out_shape=out_shape,
        compiler_params=_params(("parallel", "arbitrary")),
        name="proj_conv",
    )(h, w, halo, conv_w, ss)


def _gate_prep_kernel(gate_ref, prm_ref, gcol_ref, grow_ref, *, heads):
    c_len = DN_CHUNK
    row = lax.broadcasted_iota(jnp.int32, (c_len, c_len), 0)
    col = lax.broadcasted_iota(jnp.int32, (c_len, c_len), 1)
    tri_lo = (row >= col).astype(f32)
    tri_up = (row <= col).astype(f32)
    for c in range(gate_ref.shape[0] // c_len):
        rows = slice(c * c_len, (c + 1) * c_len)
        blk = gate_ref[rows, :]
        beta = jax.nn.sigmoid(blk)
        g = prm_ref[0:1, :] * jax.nn.softplus(blk + prm_ref[1:2, :])
        g_f = jnp.dot(tri_lo, g, preferred_element_type=f32, precision=lax.Precision.HIGHEST)
        g_b = jnp.dot(tri_up, g, preferred_element_type=f32, precision=lax.Precision.HIGHEST)
        out = jnp.where(col < 2 * heads, beta, jnp.where(col < 3 * heads, g_f, jnp.where(col < 4 * heads, g_b, 0.0)))
        gcol_ref[rows, :] = out
        grow_ref[c] = out.T


def _gate_prep(gates, prm, heads):
    t = gates.shape[0]
    n = t // DN_CHUNK
    cb = _pick(n, 8, 1)
    return pl.pallas_call(
        functools.partial(_gate_prep_kernel, heads=heads),
        grid=(n // cb,),
        in_specs=[pl.BlockSpec((cb * DN_CHUNK, LANES), lambda i: (i, 0)), pl.BlockSpec((2, LANES), lambda i: (0, 0))],
        out_specs=[pl.BlockSpec((cb * DN_CHUNK, LANES), lambda i: (i, 0)),
                   pl.BlockSpec((cb, LANES, DN_CHUNK), lambda i: (i, 0, 0))],
        out_shape=[jax.ShapeDtypeStruct((t, LANES), f32), jax.ShapeDtypeStruct((n, LANES, DN_CHUNK), f32)],
        compiler_params=_params(("parallel",)),
        name="gate_prep",
    )(gates, prm)


_M_NEG_LO, _M_NEG_UP, _M_OFFDIAG, _M_EYE, _M_LEAF, _M_OFF0 = 0, 1, 2, 3, 4, 5


def _dn_masks():
    c = DN_CHUNK
    r = np.arange(c)[:, None]
    s = np.arange(c)[None, :]
    ms = [np.where(r >= s, 0.0, -np.inf), np.where(r <= s, 0.0, -np.inf), (r != s) * 1.0, (r == s) * 1.0,
          (r // LEAF == s // LEAF) * 1.0]
    blk = LEAF
    while blk < c:
        ms.append(((r // (2 * blk) == s // (2 * blk)) & (r // blk != s // blk)) * 1.0)
        blk *= 2
    return jnp.asarray(np.stack(ms), f32)


def _block_rows(x, blk, parity):
    return jnp.concatenate([x[i * blk:(i + 1) * blk] for i in range(x.shape[0] // blk) if i % 2 == parity], axis=0)


def _merge_block_rows(base, rows, blk, parity, combine):
    out = []
    for i in range(base.shape[0] // blk):
        piece = base[i * blk:(i + 1) * blk]
        if i % 2 == parity:
            piece = combine(piece, rows[(i // 2) * blk:(i // 2 + 1) * blk])
        out.append(piece)
    return jnp.concatenate(out, axis=0)


def _tri_inverse(a_mats, lower, m_ref):
    n = a_mats[0].shape[1]
    leaf, eye = m_ref[_M_LEAF], m_ref[_M_EYE]
    ps = [-(a * leaf) for a in a_mats]
    xs = [eye + p for p in ps]
    ps = [_bdot(p, p) for p in ps]
    for _ in range(int(math.log2(LEAF)) - 2):
        pbs = [p.astype(bf16) for p in ps]
        rs = [jnp.dot(pb, jnp.concatenate([pb, x.astype(bf16)], axis=1), preferred_element_type=f32)
              for pb, x in zip(pbs, xs)]
        ps = [r[:, :n] for r in rs]
        xs = [x + r[:, n:] for x, r in zip(xs, rs)]
    ts = [x + _bdot(p, x) for p, x in zip(ps, xs)]
    parity = [1 if lo else 0 for lo in lower]
    blk = LEAF
    for lvl in range(m_ref.shape[0] - _M_OFF0):
        off = m_ref[_M_OFF0 + lvl]
        zeros = jnp.zeros((n, n), bf16)
        vs = [_bdot(_block_rows(a * off, blk, pr), t).astype(bf16) for a, t, pr in zip(a_mats, ts, parity)]
        v_full = [_merge_block_rows(zeros, v, blk, pr, lambda _, r: r) for v, pr in zip(vs, parity)]
        ds = [_bdot(_block_rows(t, blk, pr), vf) for t, vf, pr in zip(ts, v_full, parity)]
        ts = [_merge_block_rows(t, d, blk, pr, lambda p, r: p - r) for t, d, pr in zip(ts, ds, parity)]
        blk *= 2
    return ts


def _dn_kernel(qf_ref, kf_ref, vf_ref, ktf_ref, gcf_ref, grf_ref,
               qb_ref, kb_ref, vb_ref, ktb_ref, gcb_ref, grb_ref, m_ref,
               of_ref, ob_ref, st_s, *, heads, head_block, head_dim):
    c_len = DN_CHUNK
    hb = pl.program_id(1)

    @pl.when(pl.program_id(2) == 0)
    def _():
        st_s[...] = jnp.zeros_like(st_s)

    lane = lax.broadcasted_iota(jnp.int32, (c_len, LANES), 1)
    dirs = ((qf_ref, kf_ref, vf_ref, ktf_ref, gcf_ref, grf_ref, of_ref, _M_NEG_LO, c_len - 1),
            (qb_ref, kb_ref, vb_ref, ktb_ref, gcb_ref, grb_ref, ob_ref, _M_NEG_UP, 0))
    chains = [(d, hh) for d in range(2) for hh in range(head_block)]
    sls = [slice(hh * head_dim, (hh + 1) * head_dim) for _, hh in chains]

    q_c = [dirs[d][0][:, sl] for (d, _), sl in zip(chains, sls)]
    k_c = [dirs[d][1][:, sl] for (d, _), sl in zip(chains, sls)]
    kt_c = [dirs[d][3][sl, :] for (d, _), sl in zip(chains, sls)]
    states = [st_s[d, hh] for d, hh in chains]
    gram = [jnp.dot(jnp.concatenate([q, k], axis=0), kt, preferred_element_type=f32)
            for q, k, kt in zip(q_c, k_c, kt_c)]
    kqs = [jnp.dot(jnp.concatenate([k, q], axis=0), s.astype(bf16), preferred_element_type=f32)
           for q, k, s in zip(q_c, k_c, states)]

    gc, gr, br = [], [], []
    for d, hh in chains:
        head = hb * head_block + hh
        g_col = (2 + d) * heads + head
        gc.append(jnp.sum(jnp.where(lane == g_col, dirs[d][4][...], 0.0), axis=-1, keepdims=True))
        gr.append(dirs[d][5][0, pl.ds(g_col, 1), :])
        br.append(dirs[d][5][0, pl.ds(d * heads + head, 1), :])

    e = [jnp.exp(c - r + m_ref[dirs[d][7]]) * b for (d, _), c, r, b in zip(chains, gc, gr, br)]
    attn = [(g[:c_len] * ee).astype(bf16) for g, ee in zip(gram, e)]
    offdiag = m_ref[_M_OFFDIAG]
    tinv = _tri_inverse([g[c_len:] * ee * offdiag for g, ee in zip(gram, e)], [d == 0 for d, _ in chains], m_ref)

    eg = [jnp.exp(c) for c in gc]
    r = [dirs[d][2][:, sl].astype(f32) - ee * ks[:c_len] for (d, _), sl, ee, ks in zip(chains, sls, eg, kqs)]
    y = [jnp.dot(t.astype(bf16), rr.astype(bf16), preferred_element_type=f32).astype(bf16)
         for t, rr in zip(tinv, r)]
    g_end = [jnp.sum(jnp.where(lane[0:1, :] == dirs[d][8], r_, 0.0), axis=-1, keepdims=True)
             for (d, _), r_ in zip(chains, gr)]
    kt_dec = [(kt.astype(f32) * (jnp.exp(ge - r_) * b)).astype(bf16)
              for kt, ge, r_, b in zip(kt_c, g_end, gr, br)]
    ay = [jnp.dot(jnp.concatenate([a, kd], axis=0), yy, preferred_element_type=f32)
          for a, kd, yy in zip(attn, kt_dec, y)]
    for (d, hh), sl, ee, ks, ay_, ge, s in zip(chains, sls, eg, kqs, ay, g_end, states):
        dirs[d][6][:, sl] = (ee * ks[c_len:] + ay_[:c_len]).astype(dirs[d][6].dtype)
        st_s[d, hh] = jnp.exp(ge) * s + ay_[c_len:]


def _deltanet(q, k, v, kt, gcol, grow, seq, heads, head_dim):
    t = q.shape[0]
    nb = t // seq
    n = seq // DN_CHUNK
    hblk = _pick(heads, DN_HEAD_BLOCK, 1)
    w = hblk * head_dim
    fwd = lambda b, h, j: b * n + j
    bwd = lambda b, h, j: b * n + (n - 1 - j)

    def specs(pos):
        tok = pl.BlockSpec((DN_CHUNK, w), lambda b, h, j: (pos(b, h, j), h))
        return [tok, tok, tok,
                pl.BlockSpec((w, DN_CHUNK), lambda b, h, j: (h, pos(b, h, j))),
                pl.BlockSpec((DN_CHUNK, LANES), lambda b, h, j: (pos(b, h, j), 0)),
                pl.BlockSpec((1, LANES, DN_CHUNK), lambda b, h, j: (pos(b, h, j), 0, 0))]

    masks = _dn_masks()
    out_sds = jax.ShapeDtypeStruct((t, heads * head_dim), bf16)
    return pl.pallas_call(
        functools.partial(_dn_kernel, heads=heads, head_block=hblk, head_dim=head_dim),
        grid=(nb, heads // hblk, n),
        in_specs=specs(fwd) + specs(bwd) + [pl.BlockSpec(masks.shape, lambda b, h, j: (0, 0, 0))],
        out_specs=[pl.BlockSpec((DN_CHUNK, w), lambda b, h, j: (fwd(b, h, j), h)),
                   pl.BlockSpec((DN_CHUNK, w), lambda b, h, j: (bwd(b, h, j), h))],
        out_shape=[out_sds, out_sds],
        scratch_shapes=[pltpu.VMEM((2, hblk, head_dim, head_dim), f32)],
        compiler_params=_params(("parallel", "parallel", "arbitrary")),
        name="deltanet",
    )(q, k, v, kt, gcol, grow, q, k, v, kt, gcol, grow, masks)


def _mix_kernel(of_ref, ob_ref, z_ref, dng_ref, u_ref, v_ref, ws_ref, bias_ref, sgg_ref, o_ref, *,
                dn_heads, dn_dim, sg_heads, sg_dim, sg_chunk):
    dnw = dn_heads * dn_dim
    for hh in range(dn_heads):
        sl = slice(hh * dn_dim, (hh + 1) * dn_dim)
        o = of_ref[:, sl].astype(f32) + ob_ref[:, sl].astype(f32)
        ms = jnp.mean(o * o, axis=-1, keepdims=True)
        z = z_ref[:, sl].astype(f32)
        o_ref[:, sl] = (o * lax.rsqrt(ms + NORM_EPS) * dng_ref[...] * (z * jax.nn.sigmoid(z))).astype(o_ref.dtype)
    for c in range(o_ref.shape[0] // sg_chunk):
        rows = slice(c * sg_chunk, (c + 1) * sg_chunk)
        for g in range(sg_heads):
            sl = slice(g * sg_dim, (g + 1) * sg_dim)
            v = jax.nn.gelu(v_ref[rows, sl].astype(f32))
            ms = jnp.mean(v * v, axis=-1, keepdims=True)
            vn = v * lax.rsqrt(ms + NORM_EPS) * sgg_ref[:, sl]
            mixed = jnp.dot(ws_ref[g], vn.astype(bf16), preferred_element_type=f32) + bias_ref[g]
            u = jax.nn.gelu(u_ref[rows, sl].astype(f32))
            o_ref[rows, dnw + g * sg_dim:dnw + (g + 1) * sg_dim] = (u * mixed).astype(o_ref.dtype)


def _mixer_out(o_f, o_b, z, dn_norm_g, dn_heads, p_sg, sgw, ws, bias, sg_norm_g, sg_heads, sg_chunk):
    t, dnw = o_f.shape
    tb = _pick(t, 2 * sg_chunk, sg_chunk)
    sg_dim = sgw // sg_heads
    return pl.pallas_call(
        functools.partial(_mix_kernel, dn_heads=dn_heads, dn_dim=dnw // dn_heads, sg_heads=sg_heads,
                          sg_dim=sg_dim, sg_chunk=sg_chunk),
        grid=(t // tb,),
        in_specs=[pl.BlockSpec((tb, dnw), lambda i: (i, 0)), pl.BlockSpec((tb, dnw), lambda i: (i, 0)),
                  pl.BlockSpec((tb, dnw), lambda i: (i, 0)),
                  pl.BlockSpec((1, dnw // dn_heads), lambda i: (0, 0)),
                  pl.BlockSpec((tb, sgw), lambda i: (i, 0)), pl.BlockSpec((tb, sgw), lambda i: (i, 1)),
                  pl.BlockSpec((sg_heads, sg_chunk, sg_chunk), lambda i: (0, 0, 0)),
                  pl.BlockSpec((sg_heads, sg_chunk, sg_dim), lambda i: (0, 0, 0)),
                  pl.BlockSpec((1, sgw), lambda i: (0, 0))],
        out_specs=pl.BlockSpec((tb, dnw + sgw), lambda i: (i, 0)),
        out_shape=jax.ShapeDtypeStruct((t, dnw + sgw), bf16),
        compiler_params=_params(("parallel",)),
        name="mixer_out",
    )(o_f, o_b, z, dn_norm_g, p_sg, p_sg, ws, bias, sg_norm_g)


def _up_kernel(x_ref, wg_ref, wu_ref, halo_ref, cw_ref, ss_ref, o_ref, *, nsplit):
    x = x_ref[...]
    w = o_ref.shape[1] // nsplit
    rs = _row_scale(ss_ref, x.shape[1], w)
    for s in range(nsplit):
        cs = slice(s * w, (s + 1) * w)
        g = jnp.dot(x, wg_ref[:, cs], preferred_element_type=f32) * rs
        u = jnp.dot(x, wu_ref[:, cs], preferred_element_type=f32) * rs
        gp, gn = _shifted(g, halo_ref[0:1, cs], halo_ref[1:2, cs])
        y = gp * cw_ref[0:1, cs] + g * cw_ref[1:2, cs] + gn * cw_ref[2:3, cs]
        o_ref[:, cs] = (y * jax.nn.sigmoid(y) * u).astype(o_ref.dtype)


def _up_conv_gate(h, ss, w_up, conv_w, ffp, seq, tm=1024, tn=512):
    t, k = h.shape
    tm = _pick(seq, tm, BF16_SUBLANES)
    tn = _pick(ffp, tn, LANES)
    halo = _matmul(_tile_halo_rows(h, tm, seq), w_up, f32, scale=_tile_halo_rows(ss, tm, seq), n_cols=ffp,
                   name="up_halo")
    uoff = ffp // tn
    return pl.pallas_call(
        functools.partial(_up_kernel, nsplit=max(1, tn // 256)),
        grid=(t // tm, ffp // tn),
        in_specs=[pl.BlockSpec((tm, k), lambda i, j: (i, 0)),
                  pl.BlockSpec((k, tn), lambda i, j: (0, j)),
                  pl.BlockSpec((k, tn), lambda i, j: (0, j + uoff)),
                  pl.BlockSpec((8, tn), lambda i, j: (i, j)),
                  pl.BlockSpec((3, tn), lambda i, j: (0, j)),
                  pl.BlockSpec((tm, LANES), lambda i, j: (i, 0))],
        out_specs=pl.BlockSpec((tm, tn), lambda i, j: (i, j)),
        out_shape=jax.ShapeDtypeStruct((t, ffp), bf16),
        compiler_params=_params(("parallel", "arbitrary")),
        name="up_conv_gate",
    )(h, w_up, w_up, halo, conv_w, ss)


def _cast_kernel(x_ref, g_ref, o_ref, *, zero_blocks, axis):
    blk = pl.program_id(axis)
    is_zero = functools.reduce(jnp.logical_or, [blk == z for z in zero_blocks], False)
    x = x_ref[...]
    if g_ref is not None:
        x = x * jnp.concatenate([g_ref[...]] * (x.shape[1] // LANES), axis=1)
    o_ref[...] = jnp.where(is_zero, 0.0, x).astype(o_ref.dtype)


def _cast_pad(w, layer, axis, segments, pad_to=None, row_gain=None):
    _, rows, cols = w.shape
    blk = functools.reduce(math.gcd, [v for seg in segments for v in seg if v] + [1024])
    src_of, zero_blocks = [], []
    for start, length in segments:
        padded = _round_up(length, pad_to) if pad_to else length
        for b in range(padded // blk):
            if b * blk < length:
                src_of.append((start + b * blk) // blk)
            else:
                zero_blocks.append(len(src_of))
                src_of.append(src_of[-1])
    nblk = len(src_of)
    table = jnp.asarray(src_of, jnp.int32)
    if axis == 1:
        tr = _pick(rows, CAST_BLOCK_ELEMS // blk, 8)
        grid = (rows // tr, nblk)
        in_spec = pl.BlockSpec((None, tr, blk), lambda i, j, tab: (layer, i, tab[j]))
        out_spec = pl.BlockSpec((tr, blk), lambda i, j, tab: (i, j))
        out_shape = (rows, nblk * blk)
    else:
        grid = (1, nblk)
        in_spec = pl.BlockSpec((None, blk, cols), lambda i, j, tab: (layer, tab[j], 0))
        out_spec = pl.BlockSpec((blk, cols), lambda i, j, tab: (j, 0))
        out_shape = (nblk * blk, cols)

    in_specs, args = [in_spec], [table, w]
    if row_gain is not None:
        assert axis == 1
        in_specs.append(pl.BlockSpec((tr, LANES), lambda i, j, tab: (i, 0)))
        args.append(jnp.broadcast_to(row_gain.astype(f32)[:, None], (rows, LANES)))

    def body(tab_ref, x_ref, *rest):
        _cast_kernel(x_ref, rest[0] if row_gain is not None else None, rest[-1], zero_blocks=zero_blocks, axis=1)

    return pl.pallas_call(
        body,
        grid_spec=pltpu.PrefetchScalarGridSpec(num_scalar_prefetch=1, grid=grid, in_specs=in_specs,
                                               out_specs=out_spec),
        out_shape=jax.ShapeDtypeStruct(out_shape, bf16),
        compiler_params=_params(("parallel", "parallel")),
        name="cast_pad",
    )(*args)


def _cast_transpose_kernel(a_ref, b_ref, g_ref, o_ref):
    x = jnp.concatenate([a_ref[...], b_ref[...]], axis=0) * g_ref[...]
    o_ref[...] = x.T.astype(o_ref.dtype)


def _cast_transpose(w_t, layer, row_start, n_rows, col_gain):
    half = LANES // 2
    assert row_start % half == 0 and n_rows % LANES == 0
    k = w_t.shape[2]
    base = row_start // half
    piece = lambda d: pl.BlockSpec((None, half, k), lambda j: (layer, base + 2 * j + d, 0))
    return pl.pallas_call(
        _cast_transpose_kernel,
        grid=(n_rows // LANES,),
        in_specs=[piece(0), piece(1), pl.BlockSpec((1, k), lambda j: (0, 0))],
        out_specs=pl.BlockSpec((k, LANES), lambda j: (0, j)),
        out_shape=jax.ShapeDtypeStruct((k, n_rows), bf16),
        compiler_params=_params(("parallel",)),
        name="cast_transpose",
    )(w_t, w_t, col_gain.astype(f32).reshape(1, k))


def _prepare_weights(norm1_g, w_in, qkv_conv_w, a_log, dt_bias, dn_norm_g, sg_norm_g, sg_w, sg_b,
                     w_out, norm2_g, w_up, ffn_conv_w, w_down):
    depth = w_in.shape[0]
    heads = a_log.shape[-1]
    dnw = qkv_conv_w.shape[-1] // 3
    sgw = sg_norm_g.shape[-1]
    ff = ffn_conv_w.shape[-1]
    ffp = _round_up(ff, 1024)
    pad_f = ffp - ff
    g0 = 4 * dnw
    g1 = g0 + 4 * heads
    assert 4 * heads <= LANES
    sg_heads, sg_chunk = sg_w.shape[1], sg_w.shape[2]
    dims = dict(heads=heads, dnw=dnw, sgw=sgw, ffp=ffp, sg_heads=sg_heads, sg_chunk=sg_chunk)
    lead = jnp.zeros((2 * heads,), f32)
    tail = jnp.zeros((LANES - 4 * heads,), f32)
    layers = []
    w_in_t = jnp.swapaxes(w_in, 1, 2)
    for l in range(depth):
        neg_a = -jnp.exp(a_log[l].astype(f32)).reshape(2 * heads)
        dtb = dt_bias[l].astype(f32).reshape(2 * heads)
        layers.append(dict(
            w_qkvz=_cast_transpose(w_in_t, l, 0, g0, norm1_g[l]),
            w_sg=_cast_transpose(w_in_t, l, g1, 2 * sgw, norm1_g[l]),
            w_gate=_cast_transpose(w_in_t, l, g0, LANES, norm1_g[l]),
            qkv_cw=qkv_conv_w[l].astype(f32),
            prm=jnp.stack([jnp.concatenate([lead, neg_a, tail]), jnp.concatenate([lead, dtb, tail])]),
            dn_norm_g=dn_norm_g[l].astype(f32).reshape(1, -1),
            sg_norm_g=sg_norm_g[l].astype(f32).reshape(1, -1),
            sg_w=sg_w[l].astype(bf16),
            sg_bias=jnp.broadcast_to(sg_b[l].astype(f32)[..., None], (sg_heads, sg_chunk, sgw // sg_heads)),
            w_out=_cast_pad(w_out, l, 1, [(0, w_out.shape[2])]),
            w_up=_cast_pad(w_up, l, 1, [(0, ff), (ff, ff)], pad_to=1024, row_gain=norm2_g[l]),
            ffn_cw=jnp.pad(ffn_conv_w[l].astype(f32), ((0, 0), (0, pad_f))),
            w_down=_cast_pad(w_down, l, 0, [(0, ff)], pad_to=1024),
        ))
    return layers, dims


def _trunk(x, layers, dims, final_norm_g):
    nb, seq, d = x.shape
    heads, dnw, sgw, ffp = dims["heads"], dims["dnw"], dims["sgw"], dims["ffp"]
    head_dim = dnw // heads
    xt = x.reshape(nb * seq, d)
    xb, ss = _rowstats(xt)
    tm = _pick(seq, 1024, BF16_SUBLANES)
    for w in layers:
        gates = _matmul(xb, w["w_gate"], f32, scale=ss, name="gate_proj")
        halo = _matmul(_tile_halo_rows(xb, tm, seq), w["w_qkvz"], f32, scale=_tile_halo_rows(ss, tm, seq),
                       n_cols=3 * dnw, name="qkv_halo")
        cw = w["qkv_cw"]
        (q,) = _proj_conv(xb, ss, w["w_qkvz"], halo, cw[:, :dnw], 0, dnw, tm, head_dim, True, head_dim ** -0.5,
                          False)
        k, kt = _proj_conv(xb, ss, w["w_qkvz"], halo, cw[:, dnw:2 * dnw], dnw, dnw, tm, head_dim, True, 1.0, True)
        (v,) = _proj_conv(xb, ss, w["w_qkvz"], halo, cw[:, 2 * dnw:], 2 * dnw, dnw, tm, head_dim, False, 1.0,
                          False)
        z = _matmul(xb, w["w_qkvz"], bf16, scale=ss, col_off=3 * dnw, name="z_proj")
        p_sg = _matmul(xb, w["w_sg"], bf16, scale=ss, name="sg_proj")
        gcol, grow = _gate_prep(gates, w["prm"], heads)
        o_f, o_b = _deltanet(q, k, v, kt, gcol, grow, seq, heads, head_dim)
        o = _mixer_out(o_f, o_b, z, w["dn_norm_g"], heads, p_sg, sgw,
                       w["sg_w"], w["sg_bias"], w["sg_norm_g"], dims["sg_heads"], dims["sg_chunk"])
        xt, xb, ss = _matmul(o, w["w_out"], f32, res=xt, emit_stats=True, tn=512, name="out_proj")
        hid = _up_conv_gate(xb, ss, w["w_up"], w["ffn_cw"], ffp, seq)
        xt, xb, ss = _matmul(hid, w["w_down"], f32, res=xt, emit_stats=True, tk=2816, name="down_proj")
    return _rmsnorm(xt, final_norm_g, f32).reshape(nb, seq, d)


def kernel(x_prompt, x_sample, norm1_g, w_in, qkv_conv_w, a_log, dt_bias, dn_norm_g, sg_norm_g, sg_w, sg_b, w_out, norm2_g, w_up, ffn_conv_w, w_down, final_norm_g):
    layers, dims = _prepare_weights(norm1_g, w_in, qkv_conv_w, a_log, dt_bias, dn_norm_g, sg_norm_g, sg_w, sg_b,
                                    w_out, norm2_g, w_up, ffn_conv_w, w_down)
    return (_trunk(x_prompt, layers, dims, final_norm_g), _trunk(x_sample, layers, dims, final_norm_g))
```

```python
import functools
import math

import jax
import jax.numpy as jnp
import numpy as np
from jax import lax
from jax.experimental import pallas as pl
from jax.experimental.pallas import tpu as pltpu

NORM_EPS = 1e-6
L2_EPS = 1e-6
LANES = 128
BF16_SUBLANES = 16
DN_CHUNK = 128
LEAF = 16
DN_HEAD_BLOCK = 16
VMEM_LIMIT = 56 * 1024 * 1024
CAST_BLOCK_ELEMS = 1024 * 1024

f32 = jnp.float32
bf16 = jnp.bfloat16


def _pick(n, target, mult):
    best = None
    d = mult
    while d <= min(n, target):
        if n % d == 0:
            best = d
        d += mult
    return best if best is not None else n


def _round_up(n, m):
    return (n + m - 1) // m * m


def _params(sem):
    return pltpu.CompilerParams(dimension_semantics=sem, vmem_limit_bytes=VMEM_LIMIT)


def _bdot(a, b):
    return jnp.dot(a.astype(bf16), b.astype(bf16), preferred_element_type=f32)


def _rmsnorm_kernel(x_ref, g_ref, o_ref):
    x = x_ref[...]
    ms = jnp.mean(x * x, axis=-1, keepdims=True)
    o_ref[...] = (x * lax.rsqrt(ms + NORM_EPS) * g_ref[...]).astype(o_ref.dtype)


def _rmsnorm(x, g, out_dtype):
    t, d = x.shape
    tm = _pick(t, 256, 8)
    return pl.pallas_call(
        _rmsnorm_kernel,
        grid=(t // tm,),
        in_specs=[pl.BlockSpec((tm, d), lambda i: (i, 0)), pl.BlockSpec((1, d), lambda i: (0, 0))],
        out_specs=pl.BlockSpec((tm, d), lambda i: (i, 0)),
        out_shape=jax.ShapeDtypeStruct((t, d), out_dtype),
        compiler_params=_params(("parallel",)),
        name="rmsnorm",
    )(x, g.reshape(1, d).astype(f32))


def _rowstats_kernel(x_ref, xb_ref, ss_ref):
    x = x_ref[...]
    xb_ref[...] = x.astype(xb_ref.dtype)
    ss_ref[...] = jnp.broadcast_to(jnp.sum(x * x, axis=-1, keepdims=True), ss_ref.shape)


def _rowstats(x):
    t, d = x.shape
    tm = _pick(t, 256, 8)
    return pl.pallas_call(
        _rowstats_kernel,
        grid=(t // tm,),
        in_specs=[pl.BlockSpec((tm, d), lambda i: (i, 0))],
        out_specs=[pl.BlockSpec((tm, d), lambda i: (i, 0)), pl.BlockSpec((tm, LANES), lambda i: (i, 0))],
        out_shape=[jax.ShapeDtypeStruct((t, d), bf16), jax.ShapeDtypeStruct((t, LANES), f32)],
        compiler_params=_params(("parallel",)),
        name="rowstats",
    )(x)


def _row_scale(ss_ref, d, width):
    r = lax.rsqrt(ss_ref[...] * (1.0 / d) + NORM_EPS)
    return r if width == LANES else jnp.concatenate([r] * (width // LANES), axis=1)


def _mm_kernel(*refs, nk, has_res, has_scale, emit_stats, d_norm):
    refs = list(refs)
    a_ref, w_ref = refs[:2]
    pos = 2
    r_ref = s_ref = None
    if has_res:
        r_ref = refs[pos]
        pos += 1
    if has_scale:
        s_ref = refs[pos]
        pos += 1
    o_ref = refs[pos]
    pos += 1
    xb_ref = ss_ref = None
    if emit_stats:
        xb_ref, ss_ref = refs[pos:pos + 2]
        pos += 2
    scratch = refs[pos:]

    def finish(out):
        if s_ref is not None:
            out = out * _row_scale(s_ref, d_norm, out.shape[1])
        if r_ref is not None:
            out = out + r_ref[...]
        o_ref[...] = out.astype(o_ref.dtype)
        if emit_stats:
            xb_ref[...] = out.astype(xb_ref.dtype)
            part = jnp.broadcast_to(jnp.sum(out * out, axis=-1, keepdims=True), ss_ref.shape)
            j = pl.program_id(1)

            @pl.when(j == 0)
            def _():
                ss_ref[...] = part

            @pl.when(j > 0)
            def _():
                ss_ref[...] += part

    part = jnp.dot(a_ref[...], w_ref[...], preferred_element_type=f32)
    if nk == 1:
        finish(part)
        return
    acc_ref = scratch[0]
    k = pl.program_id(2)

    @pl.when(k == 0)
    def _():
        acc_ref[...] = part

    @pl.when(k > 0)
    def _():
        acc_ref[...] += part

    @pl.when(k == nk - 1)
    def _():
        finish(acc_ref[...])


def _matmul(a, w, out_dtype, res=None, scale=None, emit_stats=False, tm=1024, tn=1024, tk=4096, col_off=0,
            n_cols=None, name="matmul"):
    m, k = a.shape
    n = w.shape[1] - col_off if n_cols is None else n_cols
    tm = _pick(m, tm, 8)
    tn = _pick(math.gcd(n, col_off) if col_off else n, tn, LANES)
    tk = _pick(k, tk, LANES)
    nk = k // tk
    off = col_off // tn
    in_specs = [pl.BlockSpec((tm, tk), lambda i, j, kk: (i, kk)),
                pl.BlockSpec((tk, tn), lambda i, j, kk: (kk, j + off))]
    args = [a, w]
    if res is not None:
        in_specs.append(pl.BlockSpec((tm, tn), lambda i, j, kk: (i, j)))
        args.append(res)
    if scale is not None:
        in_specs.append(pl.BlockSpec((tm, LANES), lambda i, j, kk: (i, 0)))
        args.append(scale)
    out_specs = [pl.BlockSpec((tm, tn), lambda i, j, kk: (i, j))]
    out_shape = [jax.ShapeDtypeStruct((m, n), out_dtype)]
    if emit_stats:
        out_specs += [pl.BlockSpec((tm, tn), lambda i, j, kk: (i, j)), pl.BlockSpec((tm, LANES), lambda i, j, kk: (i, 0))]
        out_shape += [jax.ShapeDtypeStruct((m, n), bf16), jax.ShapeDtypeStruct((m, LANES), f32)]
    scratch = [pltpu.VMEM((tm, tn), f32)] if nk > 1 else []
    outs = pl.pallas_call(
        functools.partial(_mm_kernel, nk=nk, has_res=res is not None, has_scale=scale is not None,
                          emit_stats=emit_stats, d_norm=k),
        grid=(m // tm, n // tn, nk),
        in_specs=in_specs,
        out_specs=out_specs,
        out_shape=out_shape,
        scratch_shapes=scratch,
        compiler_params=_params(("parallel", "arbitrary", "arbitrary")),
        name=name,
    )(*args)
    return outs if emit_stats else outs[0]


def _shifted(x, prev_row, next_row):
    ts = x.shape[0]
    rows = lax.broadcasted_iota(jnp.int32, x.shape, 0)
    xp = jnp.where(rows == 0, prev_row, pltpu.roll(x, 1, axis=0))
    xn = jnp.where(rows == ts - 1, next_row, pltpu.roll(x, ts - 1, axis=0))
    return xp, xn


def _tile_halo_rows(h, tm, seq):
    t, k = h.shape
    nt = t // tm
    h3 = h.reshape(nt, tm, k)
    zero = jnp.zeros((1, k), h.dtype)
    prev = jnp.concatenate([zero, h3[:-1, tm - 1]], axis=0)
    nxt = jnp.concatenate([h3[1:, 0], zero], axis=0)
    pos = jnp.arange(nt) % (seq // tm)
    prev = jnp.where((pos == 0)[:, None], jnp.zeros_like(prev), prev)
    nxt = jnp.where((pos == seq // tm - 1)[:, None], jnp.zeros_like(nxt), nxt)
    pad = jnp.zeros((nt, 6, k), h.dtype)
    return jnp.concatenate([prev[:, None], nxt[:, None], pad], axis=1).reshape(nt * 8, k)


def _proj_conv_kernel(x_ref, w_ref, halo_ref, cw_ref, ss_ref, *out_refs, nsplit, normalize, scale, head_dim):
    x = x_ref[...]
    o_ref = out_refs[0]
    w = o_ref.shape[1] // nsplit
    rs = _row_scale(ss_ref, x.shape[1], w)
    for s in range(nsplit):
        cs = slice(s * w, (s + 1) * w)
        pre = jnp.dot(x, w_ref[:, cs], preferred_element_type=f32) * rs
        pp, pn = _shifted(pre, halo_ref[0:1, cs], halo_ref[1:2, cs])
        y = pp * cw_ref[0:1, cs] + pre * cw_ref[1:2, cs] + pn * cw_ref[2:3, cs]
        y = y * jax.nn.sigmoid(y)
        for hh in range(w // head_dim):
            seg = y[:, hh * head_dim:(hh + 1) * head_dim]
            if normalize:
                ss = jnp.sum(seg * seg, axis=-1, keepdims=True)
                seg = seg * lax.rsqrt(ss + L2_EPS)
                if scale != 1.0:
                    seg = seg * scale
            hs = slice(s * w + hh * head_dim, s * w + (hh + 1) * head_dim)
            o_ref[:, hs] = seg.astype(o_ref.dtype)
            if len(out_refs) > 1:
                out_refs[1][hs, :] = seg.T.astype(out_refs[1].dtype)


def _proj_conv(h, ss, w, halo, conv_w, col_off, width, tm, head_dim, normalize, scale, transpose_out, tn=1024):
    t, k = h.shape
    tn = _pick(math.gcd(width, col_off) if col_off else width, tn, LANES)
    off = col_off // tn
    out_shape = [jax.ShapeDtypeStruct((t, width), bf16)]
    out_specs = [pl.BlockSpec((tm, tn), lambda i, j: (i, j))]
    if transpose_out:
        out_shape.append(jax.ShapeDtypeStruct((width, t), bf16))
        out_specs.append(pl.BlockSpec((tn, tm), lambda i, j: (j, i)))
    return pl.pallas_call(
        functools.partial(_proj_conv_kernel, nsplit=max(1, tn // 256), normalize=normalize, scale=scale,
                          head_dim=head_dim),
        grid=(t // tm, width // tn),
        in_specs=[pl.BlockSpec((tm, k), lambda i, j: (i, 0)),
                  pl.BlockSpec((k, tn), lambda i, j: (0, j + off)),
                  pl.BlockSpec((8, tn), lambda i, j: (i, j + off)),
                  pl.BlockSpec((3, tn), lambda i, j: (0, j)),
                  pl.BlockSpec((tm, LANES), lambda i, j: (i, 0))],
        out_specs=out_specs,
        out_shape=out_shape,
        compiler_params=_params(("parallel", "arbitrary")),
        name="proj_conv",
    )(h, w, halo, conv_w, ss)


def _gate_prep_kernel(gate_ref, prm_ref, gcol_ref, grow_ref, *, heads):
    c_len = DN_CHUNK
    row = lax.broadcasted_iota(jnp.int32, (c_len, c_len), 0)
    col = lax.broadcasted_iota(jnp.int32, (c_len, c_len), 1)
    tri_lo = (row >= col).astype(f32)
    tri_up = (row <= col).astype(f32)
    for c in range(gate_ref.shape[0] // c_len):
        rows = slice(c * c_len, (c + 1) * c_len)
        blk = gate_ref[rows, :]
        beta = jax.nn.sigmoid(blk)
        g = prm_ref[0:1, :] * jax.nn.softplus(blk + prm_ref[1:2, :])
        g_f = jnp.dot(tri_lo, g, preferred_element_type=f32, precision=lax.Precision.HIGHEST)
        g_b = jnp.dot(tri_up, g, preferred_element_type=f32, precision=lax.Precision.HIGHEST)
        out = jnp.where(col < 2 * heads, beta, jnp.where(col < 3 * heads, g_f, jnp.where(col < 4 * heads, g_b, 0.0)))
        gcol_ref[rows, :] = out
        grow_ref[c] = out.T


def _gate_prep(gates, prm, heads):
    t = gates.shape[0]
    n = t // DN_CHUNK
    cb = _pick(n, 8, 1)
    return pl.pallas_call(
        functools.partial(_gate_prep_kernel, heads=heads),
        grid=(n // cb,),
        in_specs=[pl.BlockSpec((cb * DN_CHUNK, LANES), lambda i: (i, 0)), pl.BlockSpec((2, LANES), lambda i: (0, 0))],
        out_specs=[pl.BlockSpec((cb * DN_CHUNK, LANES), lambda i: (i, 0)),
                   pl.BlockSpec((cb, LANES, DN_CHUNK), lambda i: (i, 0, 0))],
        out_shape=[jax.ShapeDtypeStruct((t, LANES), f32), jax.ShapeDtypeStruct((n, LANES, DN_CHUNK), f32)],
        compiler_params=_params(("parallel",)),
        name="gate_prep",
    )(gates, prm)


_M_NEG_LO, _M_NEG_UP, _M_OFFDIAG, _M_EYE, _M_LEAF, _M_OFF0 = 0, 1, 2, 3, 4, 5


def _dn_masks():
    c = DN_CHUNK
    r = np.arange(c)[:, None]
    s = np.arange(c)[None, :]
    ms = [np.where(r >= s, 0.0, -np.inf), np.where(r <= s, 0.0, -np.inf), (r != s) * 1.0, (r == s) * 1.0,
          (r // LEAF == s // LEAF) * 1.0]
    blk = LEAF
    while blk < c:
        ms.append(((r // (2 * blk) == s // (2 * blk)) & (r // blk != s // blk)) * 1.0)
        blk *= 2
    return jnp.asarray(np.stack(ms), f32)


def _block_rows(x, blk, parity):
    return jnp.concatenate([x[i * blk:(i + 1) * blk] for i in range(x.shape[0] // blk) if i % 2 == parity], axis=0)


def _merge_block_rows(base, rows, blk, parity, combine):
    out = []
    for i in range(base.shape[0] // blk):
        piece = base[i * blk:(i + 1) * blk]
        if i % 2 == parity:
            piece = combine(piece, rows[(i // 2) * blk:(i // 2 + 1) * blk])
        out.append(piece)
    return jnp.concatenate(out, axis=0)


def _tri_inverse(a_mats, lower, m_ref):
    n = a_mats[0].shape[1]
    leaf, eye = m_ref[_M_LEAF], m_ref[_M_EYE]
    ps = [-(a * leaf) for a in a_mats]
    xs = [eye + p for p in ps]
    ps = [_bdot(p, p) for p in ps]
    for _ in range(int(math.log2(LEAF)) - 2):
        pbs = [p.astype(bf16) for p in ps]
        rs = [jnp.dot(pb, jnp.concatenate([pb, x.astype(bf16)], axis=1), preferred_element_type=f32)
              for pb, x in zip(pbs, xs)]
        ps = [r[:, :n] for r in rs]
        xs = [x + r[:, n:] for x, r in zip(xs, rs)]
    ts = [x + _bdot(p, x) for p, x in zip(ps, xs)]
    parity = [1 if lo else 0 for lo in lower]
    blk = LEAF
    for lvl in range(m_ref.shape[0] - _M_OFF0):
        off = m_ref[_M_OFF0 + lvl]
        zeros = jnp.zeros((n, n), bf16)
        vs = [_bdot(_block_rows(a * off, blk, pr), t).astype(bf16) for a, t, pr in zip(a_mats, ts, parity)]
        v_full = [_merge_block_rows(zeros, v, blk, pr, lambda _, r: r) for v, pr in zip(vs, parity)]
        ds = [_bdot(_block_rows(t, blk, pr), vf) for t, vf, pr in zip(ts, v_full, parity)]
        ts = [_merge_block_rows(t, d, blk, pr, lambda p, r: p - r) for t, d, pr in zip(ts, ds, parity)]
        blk *= 2
    return ts


def _dn_kernel(qf_ref, kf_ref, vf_ref, ktf_ref, gcf_ref, grf_ref,
               qb_ref, kb_ref, vb_ref, ktb_ref, gcb_ref, grb_ref, m_ref,
               of_ref, ob_ref, st_s, *, heads, head_block, head_dim):
    c_len = DN_CHUNK
    hb = pl.program_id(1)

    @pl.when(pl.program_id(2) == 0)
    def _():
        st_s[...] = jnp.zeros_like(st_s)

    lane = lax.broadcasted_iota(jnp.int32, (c_len, LANES), 1)
    dirs = ((qf_ref, kf_ref, vf_ref, ktf_ref, gcf_ref, grf_ref, of_ref, _M_NEG_LO, c_len - 1),
            (qb_ref, kb_ref, vb_ref, ktb_ref, gcb_ref, grb_ref, ob_ref, _M_NEG_UP, 0))
    chains = [(d, hh) for d in range(2) for hh in range(head_block)]
    sls = [slice(hh * head_dim, (hh + 1) * head_dim) for _, hh in chains]

    q_c = [dirs[d][0][:, sl] for (d, _), sl in zip(chains, sls)]
    k_c = [dirs[d][1][:, sl] for (d, _), sl in zip(chains, sls)]
    kt_c = [dirs[d][3][sl, :] for (d, _), sl in zip(chains, sls)]
    states = [st_s[d, hh] for d, hh in chains]
    gram = [jnp.dot(jnp.concatenate([q, k], axis=0), kt, preferred_element_type=f32)
            for q, k, kt in zip(q_c, k_c, kt_c)]
    kqs = [jnp.dot(jnp.concatenate([k, q], axis=0), s.astype(bf16), preferred_element_type=f32)
           for q, k, s in zip(q_c, k_c, states)]

    gc, gr, br = [], [], []
    for d, hh in chains:
        head = hb * head_block + hh
        g_col = (2 + d) * heads + head
        gc.append(jnp.sum(jnp.where(lane == g_col, dirs[d][4][...], 0.0), axis=-1, keepdims=True))
        gr.append(dirs[d][5][0, pl.ds(g_col, 1), :])
        br.append(dirs[d][5][0, pl.ds(d * heads + head, 1), :])

    e = [jnp.exp(c - r + m_ref[dirs[d][7]]) * b for (d, _), c, r, b in zip(chains, gc, gr, br)]
    attn = [(g[:c_len] * ee).astype(bf16) for g, ee in zip(gram, e)]
    offdiag = m_ref[_M_OFFDIAG]
    tinv = _tri_inverse([g[c_len:] * ee * offdiag for g, ee in zip(gram, e)], [d == 0 for d, _ in chains], m_ref)

    eg = [jnp.exp(c) for c in gc]
    r = [dirs[d][2][:, sl].astype(f32) - ee * ks[:c_len] for (d, _), sl, ee, ks in zip(chains, sls, eg, kqs)]
    y = [jnp.dot(t.astype(bf16), rr.astype(bf16), preferred_element_type=f32).astype(bf16)
         for t, rr in zip(tinv, r)]
    g_end = [jnp.sum(jnp.where(lane[0:1, :] == dirs[d][8], r_, 0.0), axis=-1, keepdims=True)
             for (d, _), r_ in zip(chains, gr)]
    kt_dec = [(kt.astype(f32) * (jnp.exp(ge - r_) * b)).astype(bf16)
              for kt, ge, r_, b in zip(kt_c, g_end, gr, br)]
    ay = [jnp.dot(jnp.concatenate([a, kd], axis=0), yy, preferred_element_type=f32)
          for a, kd, yy in zip(attn, kt_dec, y)]
    for (d, hh), sl, ee, ks, ay_, ge, s in zip(chains, sls, eg, kqs, ay, g_end, states):
        dirs[d][6][:, sl] = (ee * ks[c_len:] + ay_[:c_len]).astype(dirs[d][6].dtype)
        st_s[d, hh] = jnp.exp(ge) * s + ay_[c_len:]


def _deltanet(q, k, v, kt, gcol, grow, seq, heads, head_dim):
    t = q.shape[0]
    nb = t // seq
    n = seq // DN_CHUNK
    hblk = _pick(heads, DN_HEAD_BLOCK, 1)
    w = hblk * head_dim
    fwd = lambda b, h, j: b * n + j
    bwd = lambda b, h, j: b * n + (n - 1 - j)

    def specs(pos):
        tok = pl.BlockSpec((DN_CHUNK, w), lambda b, h, j: (pos(b, h, j), h))
        return [tok, tok, tok,
                pl.BlockSpec((w, DN_CHUNK), lambda b, h, j: (h, pos(b, h, j))),
                pl.BlockSpec((DN_CHUNK, LANES), lambda b, h, j: (pos(b, h, j), 0)),
                pl.BlockSpec((1, LANES, DN_CHUNK), lambda b, h, j: (pos(b, h, j), 0, 0))]

    masks = _dn_masks()
    out_sds = jax.ShapeDtypeStruct((t, heads * head_dim), bf16)
    return pl.pallas_call(
        functools.partial(_dn_kernel, heads=heads, head_block=hblk, head_dim=head_dim),
        grid=(nb, heads // hblk, n),
        in_specs=specs(fwd) + specs(bwd) + [pl.BlockSpec(masks.shape, lambda b, h, j: (0, 0, 0))],
        out_specs=[pl.BlockSpec((DN_CHUNK, w), lambda b, h, j: (fwd(b, h, j), h)),
                   pl.BlockSpec((DN_CHUNK, w), lambda b, h, j: (bwd(b, h, j), h))],
        out_shape=[out_sds, out_sds],
        scratch_shapes=[pltpu.VMEM((2, hblk, head_dim, head_dim), f32)],
        compiler_params=_params(("parallel", "parallel", "arbitrary")),
        name="deltanet",
    )(q, k, v, kt, gcol, grow, q, k, v, kt, gcol, grow, masks)


def _mix_kernel(of_ref, ob_ref, z_ref, dng_ref, u_ref, v_ref, ws_ref, bias_ref, sgg_ref, o_ref, *,
                dn_heads, dn_dim, sg_heads, sg_dim, sg_chunk):
    dnw = dn_heads * dn_dim
    for hh in range(dn_heads):
        sl = slice(hh * dn_dim, (hh + 1) * dn_dim)
        o = of_ref[:, sl].astype(f32) + ob_ref[:, sl].astype(f32)
        ms = jnp.mean(o * o, axis=-1, keepdims=True)
        z = z_ref[:, sl].astype(f32)
        o_ref[:, sl] = (o * lax.rsqrt(ms + NORM_EPS) * dng_ref[...] * (z * jax.nn.sigmoid(z))).astype(o_ref.dtype)
    for c in range(o_ref.shape[0] // sg_chunk):
        rows = slice(c * sg_chunk, (c + 1) * sg_chunk)
        for g in range(sg_heads):
            sl = slice(g * sg_dim, (g + 1) * sg_dim)
            v = jax.nn.gelu(v_ref[rows, sl].astype(f32))
            ms = jnp.mean(v * v, axis=-1, keepdims=True)
            vn = v * lax.rsqrt(ms + NORM_EPS) * sgg_ref[:, sl]
            mixed = jnp.dot(ws_ref[g], vn.astype(bf16), preferred_element_type=f32) + bias_ref[g]
            u = jax.nn.gelu(u_ref[rows, sl].astype(f32))
            o_ref[rows, dnw + g * sg_dim:dnw + (g + 1) * sg_dim] = (u * mixed).astype(o_ref.dtype)


def _mixer_out(o_f, o_b, z, dn_norm_g, dn_heads, p_sg, sgw, ws, bias, sg_norm_g, sg_heads, sg_chunk):
    t, dnw = o_f.shape
    tb = _pick(t, 2 * sg_chunk, sg_chunk)
    sg_dim = sgw // sg_heads
    return pl.pallas_call(
        functools.partial(_mix_kernel, dn_heads=dn_heads, dn_dim=dnw // dn_heads, sg_heads=sg_heads,
                          sg_dim=sg_dim, sg_chunk=sg_chunk),
        grid=(t // tb,),
        in_specs=[pl.BlockSpec((tb, dnw), lambda i: (i, 0)), pl.BlockSpec((tb, dnw), lambda i: (i, 0)),
                  pl.BlockSpec((tb, dnw), lambda i: (i, 0)),
                  pl.BlockSpec((1, dnw // dn_heads), lambda i: (0, 0)),
                  pl.BlockSpec((tb, sgw), lambda i: (i, 0)), pl.BlockSpec((tb, sgw), lambda i: (i, 1)),
                  pl.BlockSpec((sg_heads, sg_chunk, sg_chunk), lambda i: (0, 0, 0)),
                  pl.BlockSpec((sg_heads, sg_chunk, sg_dim), lambda i: (0, 0, 0)),
                  pl.BlockSpec((1, sgw), lambda i: (0, 0))],
        out_specs=pl.BlockSpec((tb, dnw + sgw), lambda i: (i, 0)),
        out_shape=jax.ShapeDtypeStruct((t, dnw + sgw), bf16),
        compiler_params=_params(("parallel",)),
        name="mixer_out",
    )(o_f, o_b, z, dn_norm_g, p_sg, p_sg, ws, bias, sg_norm_g)


def _up_kernel(x_ref, wg_ref, wu_ref, halo_ref, cw_ref, ss_ref, o_ref, xn_s, *, nsplit, row_chunk):
    @pl.when(pl.program_id(1) == 0)
    def _():
        for c in range(x_ref.shape[0] // row_chunk):
            rows = slice(c * row_chunk, (c + 1) * row_chunk)
            r = lax.rsqrt(ss_ref[rows, :] * (1.0 / x_ref.shape[1]) + NORM_EPS)
            r = jnp.concatenate([r] * (x_ref.shape[1] // LANES), axis=1)
            xn_s[rows, :] = (x_ref[rows, :].astype(f32) * r).astype(xn_s.dtype)

    x = xn_s[...]
    w = o_ref.shape[1] // nsplit
    for s in range(nsplit):
        cs = slice(s * w, (s + 1) * w)
        g = jnp.dot(x, wg_ref[:, cs], preferred_element_type=f32)
        u = jnp.dot(x, wu_ref[:, cs], preferred_element_type=f32)
        gp, gn = _shifted(g, halo_ref[0:1, cs], halo_ref[1:2, cs])
        y = gp * cw_ref[0:1, cs] + g * cw_ref[1:2, cs] + gn * cw_ref[2:3, cs]
        o_ref[:, cs] = (y * jax.nn.sigmoid(y) * u).astype(o_ref.dtype)


def _up_conv_gate(h, ss, w_up, conv_w, ffp, seq, tm=1024, tn=512):
    t, k = h.shape
    tm = _pick(seq, tm, BF16_SUBLANES)
    tn = _pick(ffp, tn, LANES)
    halo = _matmul(_tile_halo_rows(h, tm, seq), w_up, f32, scale=_tile_halo_rows(ss, tm, seq), n_cols=ffp,
                   name="up_halo")
    uoff = ffp // tn
    return pl.pallas_call(
        functools.partial(_up_kernel, nsplit=max(1, tn // 256), row_chunk=_pick(tm, 128, BF16_SUBLANES)),
        grid=(t // tm, ffp // tn),
        in_specs=[pl.BlockSpec((tm, k), lambda i, j: (i, 0)),
                  pl.BlockSpec((k, tn), lambda i, j: (0, j)),
                  pl.BlockSpec((k, tn), lambda i, j: (0, j + uoff)),
                  pl.BlockSpec((8, tn), lambda i, j: (i, j)),
                  pl.BlockSpec((3, tn), lambda i, j: (0, j)),
                  pl.BlockSpec((tm, LANES), lambda i, j: (i, 0))],
        out_specs=pl.BlockSpec((tm, tn), lambda i, j: (i, j)),
        out_shape=jax.ShapeDtypeStruct((t, ffp), bf16),
        scratch_shapes=[pltpu.VMEM((tm, k), bf16)],
        compiler_params=_params(("parallel", "arbitrary")),
        name="up_conv_gate",
    )(h, w_up, w_up, halo, conv_w, ss)


def _cast_kernel(x_ref, g_ref, o_ref, *, zero_blocks, axis):
    blk = pl.program_id(axis)
    is_zero = functools.reduce(jnp.logical_or, [blk == z for z in zero_blocks], False)
    x = x_ref[...]
    if g_ref is not None:
        x = x * jnp.concatenate([g_ref[...]] * (x.shape[1] // LANES), axis=1)
    o_ref[...] = jnp.where(is_zero, 0.0, x).astype(o_ref.dtype)


def _cast_pad(w, layer, axis, segments, pad_to=None, row_gain=None):
    _, rows, cols = w.shape
    blk = functools.reduce(math.gcd, [v for seg in segments for v in seg if v] + [1024])
    src_of, zero_blocks = [], []
    for start, length in segments:
        padded = _round_up(length, pad_to) if pad_to else length
        for b in range(padded // blk):
            if b * blk < length:
                src_of.append((start + b * blk) // blk)
            else:
                zero_blocks.append(len(src_of))
                src_of.append(src_of[-1])
    nblk = len(src_of)
    table = jnp.asarray(src_of, jnp.int32)
    if axis == 1:
        tr = _pick(rows, CAST_BLOCK_ELEMS // blk, 8)
        grid = (rows // tr, nblk)
        in_spec = pl.BlockSpec((None, tr, blk), lambda i, j, tab: (layer, i, tab[j]))
        out_spec = pl.BlockSpec((tr, blk), lambda i, j, tab: (i, j))
        out_shape = (rows, nblk * blk)
    else:
        grid = (1, nblk)
        in_spec = pl.BlockSpec((None, blk, cols), lambda i, j, tab: (layer, tab[j], 0))
        out_spec = pl.BlockSpec((blk, cols), lambda i, j, tab: (j, 0))
        out_shape = (nblk * blk, cols)

    in_specs, args = [in_spec], [table, w]
    if row_gain is not None:
        assert axis == 1
        in_specs.append(pl.BlockSpec((tr, LANES), lambda i, j, tab: (i, 0)))
        args.append(jnp.broadcast_to(row_gain.astype(f32)[:, None], (rows, LANES)))

    def body(tab_ref, x_ref, *rest):
        _cast_kernel(x_ref, rest[0] if row_gain is not None else None, rest[-1], zero_blocks=zero_blocks, axis=1)

    return pl.pallas_call(
        body,
        grid_spec=pltpu.PrefetchScalarGridSpec(num_scalar_prefetch=1, grid=grid, in_specs=in_specs,
                                               out_specs=out_spec),
        out_shape=jax.ShapeDtypeStruct(out_shape, bf16),
        compiler_params=_params(("parallel", "parallel")),
        name="cast_pad",
    )(*args)


def _cast_transpose_kernel(a_ref, b_ref, g_ref, o_ref):
    x = jnp.concatenate([a_ref[...], b_ref[...]], axis=0) * g_ref[...]
    o_ref[...] = x.T.astype(o_ref.dtype)


def _cast_transpose(w_t, layer, row_start, n_rows, col_gain):
    half = LANES // 2
    assert row_start % half == 0 and n_rows % LANES == 0
    k = w_t.shape[2]
    base = row_start // half
    piece = lambda d: pl.BlockSpec((None, half, k), lambda j: (layer, base + 2 * j + d, 0))
    return pl.pallas_call(
        _cast_transpose_kernel,
        grid=(n_rows // LANES,),
        in_specs=[piece(0), piece(1), pl.BlockSpec((1, k), lambda j: (0, 0))],
        out_specs=pl.BlockSpec((k, LANES), lambda j: (0, j)),
        out_shape=jax.ShapeDtypeStruct((k, n_rows), bf16),
        compiler_params=_params(("parallel",)),
        name="cast_transpose",
    )(w_t, w_t, col_gain.astype(f32).reshape(1, k))


def _prepare_weights(norm1_g, w_in, qkv_conv_w, a_log, dt_bias, dn_norm_g, sg_norm_g, sg_w, sg_b,
                     w_out, norm2_g, w_up, ffn_conv_w, w_down):
    depth = w_in.shape[0]
    heads = a_log.shape[-1]
    dnw = qkv_conv_w.shape[-1] // 3
    sgw = sg_norm_g.shape[-1]
    ff = ffn_conv_w.shape[-1]
    ffp = _round_up(ff, 1024)
    pad_f = ffp - ff
    g0 = 4 * dnw
    g1 = g0 + 4 * heads
    assert 4 * heads <= LANES
    sg_heads, sg_chunk = sg_w.shape[1], sg_w.shape[2]
    dims = dict(heads=heads, dnw=dnw, sgw=sgw, ffp=ffp, sg_heads=sg_heads, sg_chunk=sg_chunk)
    lead = jnp.zeros((2 * heads,), f32)
    tail = jnp.zeros((LANES - 4 * heads,), f32)
    layers = []
    w_in_t = jnp.swapaxes(w_in, 1, 2)
    for l in range(depth):
        neg_a = -jnp.exp(a_log[l].astype(f32)).reshape(2 * heads)
        dtb = dt_bias[l].astype(f32).reshape(2 * heads)
        layers.append(dict(
            w_qkvz=_cast_transpose(w_in_t, l, 0, g0, norm1_g[l]),
            w_sg=_cast_transpose(w_in_t, l, g1, 2 * sgw, norm1_g[l]),
            w_gate=_cast_transpose(w_in_t, l, g0, LANES, norm1_g[l]),
            qkv_cw=qkv_conv_w[l].astype(f32),
            prm=jnp.stack([jnp.concatenate([lead, neg_a, tail]), jnp.concatenate([lead, dtb, tail])]),
            dn_norm_g=dn_norm_g[l].astype(f32).reshape(1, -1),
            sg_norm_g=sg_norm_g[l].astype(f32).reshape(1, -1),
            sg_w=sg_w[l].astype(bf16),
            sg_bias=jnp.broadcast_to(sg_b[l].astype(f32)[..., None], (sg_heads, sg_chunk, sgw // sg_heads)),
            w_out=_cast_pad(w_out, l, 1, [(0, w_out.shape[2])]),
            w_up=_cast_pad(w_up, l, 1, [(0, ff), (ff, ff)], pad_to=1024, row_gain=norm2_g[l]),
            ffn_cw=jnp.pad(ffn_conv_w[l].astype(f32), ((0, 0), (0, pad_f))),
            w_down=_cast_pad(w_down, l, 0, [(0, ff)], pad_to=1024),
        ))
    return layers, dims


def _trunk(x, layers, dims, final_norm_g):
    nb, seq, d = x.shape
    heads, dnw, sgw, ffp = dims["heads"], dims["dnw"], dims["sgw"], dims["ffp"]
    head_dim = dnw // heads
    xt = x.reshape(nb * seq, d)
    xb, ss = _rowstats(xt)
    tm = _pick(seq, 1024, BF16_SUBLANES)
    for w in layers:
        gates = _matmul(xb, w["w_gate"], f32, scale=ss, name="gate_proj")
        halo = _matmul(_tile_halo_rows(xb, tm, seq), w["w_qkvz"], f32, scale=_tile_halo_rows(ss, tm, seq),
                       n_cols=3 * dnw, name="qkv_halo")
        cw = w["qkv_cw"]
        (q,) = _proj_conv(xb, ss, w["w_qkvz"], halo, cw[:, :dnw], 0, dnw, tm, head_dim, True, head_dim ** -0.5,
                          False)
        k, kt = _proj_conv(xb, ss, w["w_qkvz"], halo, cw[:, dnw:2 * dnw], dnw, dnw, tm, head_dim, True, 1.0, True)
        (v,) = _proj_conv(xb, ss, w["w_qkvz"], halo, cw[:, 2 * dnw:], 2 * dnw, dnw, tm, head_dim, False, 1.0,
                          False)
        z = _matmul(xb, w["w_qkvz"], bf16, scale=ss, col_off=3 * dnw, name="z_proj")
        p_sg = _matmul(xb, w["w_sg"], bf16, scale=ss, name="sg_proj")
        gcol, grow = _gate_prep(gates, w["prm"], heads)
        o_f, o_b = _deltanet(q, k, v, kt, gcol, grow, seq, heads, head_dim)
        o = _mixer_out(o_f, o_b, z, w["dn_norm_g"], heads, p_sg, sgw,
                       w["sg_w"], w["sg_bias"], w["sg_norm_g"], dims["sg_heads"], dims["sg_chunk"])
        xt, xb, ss = _matmul(o, w["w_out"], f32, res=xt, emit_stats=True, tn=512, name="out_proj")
        hid = _up_conv_gate(xb, ss, w["w_up"], w["ffn_cw"], ffp, seq)
        xt, xb, ss = _matmul(hid, w["w_down"], f32, res=xt, emit_stats=True, tk=2816, name="down_proj")
    return _rmsnorm(xt, final_norm_g, f32).reshape(nb, seq, d)


def kernel(x_prompt, x_sample, norm1_g, w_in, qkv_conv_w, a_log, dt_bias, dn_norm_g, sg_norm_g, sg_w, sg_b, w_out, norm2_g, w_up, ffn_conv_w, w_down, final_norm_g):
    layers, dims = _prepare_weights(norm1_g, w_in, qkv_conv_w, a_log, dt_bias, dn_norm_g, sg_norm_g, sg_w, sg_b,
                                    w_out, norm2_g, w_up, ffn_conv_w, w_down)
    return (_trunk(x_prompt, layers, dims, final_norm_g), _trunk(x_sample, layers, dims, final_norm_g))
```

```python
import functools
import math

import jax
import jax.numpy as jnp
import numpy as np
from jax import lax
from jax.experimental import pallas as pl
from jax.experimental.pallas import tpu as pltpu

NORM_EPS = 1e-6
L2_EPS = 1e-6
LANES = 128
BF16_SUBLANES = 16
DN_CHUNK = 128
LEAF = 16
DN_HEAD_BLOCK = 16
VMEM_LIMIT = 56 * 1024 * 1024
CAST_BLOCK_ELEMS = 1024 * 1024

f32 = jnp.float32
bf16 = jnp.bfloat16


def _pick(n, target, mult):
    best = None
    d = mult
    while d <= min(n, target):
        if n % d == 0:
            best = d
        d += mult
    return best if best is not None else n


def _round_up(n, m):
    return (n + m - 1) // m * m


def _params(sem):
    return pltpu.CompilerParams(dimension_semantics=sem, vmem_limit_bytes=VMEM_LIMIT)


def _bdot(a, b):
    return jnp.dot(a.astype(bf16), b.astype(bf16), preferred_element_type=f32)


def _rmsnorm_kernel(x_ref, g_ref, o_ref):
    x = x_ref[...]
    ms = jnp.mean(x * x, axis=-1, keepdims=True)
    o_ref[...] = (x * lax.rsqrt(ms + NORM_EPS) * g_ref[...]).astype(o_ref.dtype)


def _rmsnorm(x, g, out_dtype):
    t, d = x.shape
    tm = _pick(t, 256, 8)
    return pl.pallas_call(
        _rmsnorm_kernel,
        grid=(t // tm,),
        in_specs=[pl.BlockSpec((tm, d), lambda i: (i, 0)), pl.BlockSpec((1, d), lambda i: (0, 0))],
        out_specs=pl.BlockSpec((tm, d), lambda i: (i, 0)),
        out_shape=jax.ShapeDtypeStruct((t, d), out_dtype),
        compiler_params=_params(("parallel",)),
        name="rmsnorm",
    )(x, g.reshape(1, d).astype(f32))


def _rowstats_kernel(x_ref, xb_ref, ss_ref):
    x = x_ref[...]
    xb_ref[...] = x.astype(xb_ref.dtype)
    ss_ref[...] = jnp.broadcast_to(jnp.sum(x * x, axis=-1, keepdims=True), ss_ref.shape)


def _rowstats(x):
    t, d = x.shape
    tm = _pick(t, 256, 8)
    return pl.pallas_call(
        _rowstats_kernel,
        grid=(t // tm,),
        in_specs=[pl.BlockSpec((tm, d), lambda i: (i, 0))],
        out_specs=[pl.BlockSpec((tm, d), lambda i: (i, 0)), pl.BlockSpec((tm, LANES), lambda i: (i, 0))],
        out_shape=[jax.ShapeDtypeStruct((t, d), bf16), jax.ShapeDtypeStruct((t, LANES), f32)],
        compiler_params=_params(("parallel",)),
        name="rowstats",
    )(x)


def _row_scale(ss_ref, d, width):
    r = lax.rsqrt(ss_ref[...] * (1.0 / d) + NORM_EPS)
    return r if width == LANES else jnp.concatenate([r] * (width // LANES), axis=1)


def _mm_kernel(*refs, nk, has_res, has_scale, emit_stats, d_norm):
    refs = list(refs)
    a_ref, w_ref = refs[:2]
    pos = 2
    r_ref = s_ref = None
    if has_res:
        r_ref = refs[pos]
        pos += 1
    if has_scale:
        s_ref = refs[pos]
        pos += 1
    o_ref = refs[pos]
    pos += 1
    xb_ref = ss_ref = None
    if emit_stats:
        xb_ref, ss_ref = refs[pos:pos + 2]
        pos += 2
    scratch = refs[pos:]

    def finish(out):
        if s_ref is not None:
            out = out * _row_scale(s_ref, d_norm, out.shape[1])
        if r_ref is not None:
            out = out + r_ref[...]
        o_ref[...] = out.astype(o_ref.dtype)
        if emit_stats:
            xb_ref[...] = out.astype(xb_ref.dtype)
            part = jnp.broadcast_to(jnp.sum(out * out, axis=-1, keepdims=True), ss_ref.shape)
            j = pl.program_id(1)

            @pl.when(j == 0)
            def _():
                ss_ref[...] = part

            @pl.when(j > 0)
            def _():
                ss_ref[...] += part

    part = jnp.dot(a_ref[...], w_ref[...], preferred_element_type=f32)
    if nk == 1:
        finish(part)
        return
    acc_ref = scratch[0]
    k = pl.program_id(2)

    @pl.when(k == 0)
    def _():
        acc_ref[...] = part

    @pl.when(k > 0)
    def _():
        acc_ref[...] += part

    @pl.when(k == nk - 1)
    def _():
        finish(acc_ref[...])


def _matmul(a, w, out_dtype, res=None, scale=None, emit_stats=False, tm=1024, tn=1024, tk=4096, col_off=0,
            n_cols=None, name="matmul"):
    m, k = a.shape
    n = w.shape[1] - col_off if n_cols is None else n_cols
    tm = _pick(m, tm, 8)
    tn = _pick(math.gcd(n, col_off) if col_off else n, tn, LANES)
    tk = _pick(k, tk, LANES)
    nk = k // tk
    off = col_off // tn
    in_specs = [pl.BlockSpec((tm, tk), lambda i, j, kk: (i, kk)),
                pl.BlockSpec((tk, tn), lambda i, j, kk: (kk, j + off))]
    args = [a, w]
    if res is not None:
        in_specs.append(pl.BlockSpec((tm, tn), lambda i, j, kk: (i, j)))
        args.append(res)
    if scale is not None:
        in_specs.append(pl.BlockSpec((tm, LANES), lambda i, j, kk: (i, 0)))
        args.append(scale)
    out_specs = [pl.BlockSpec((tm, tn), lambda i, j, kk: (i, j))]
    out_shape = [jax.ShapeDtypeStruct((m, n), out_dtype)]
    if emit_stats:
        out_specs += [pl.BlockSpec((tm, tn), lambda i, j, kk: (i, j)), pl.BlockSpec((tm, LANES), lambda i, j, kk: (i, 0))]
        out_shape += [jax.ShapeDtypeStruct((m, n), bf16), jax.ShapeDtypeStruct((m, LANES), f32)]
    scratch = [pltpu.VMEM((tm, tn), f32)] if nk > 1 else []
    outs = pl.pallas_call(
        functools.partial(_mm_kernel, nk=nk, has_res=res is not None, has_scale=scale is not None,
                          emit_stats=emit_stats, d_norm=k),
        grid=(m // tm, n // tn, nk),
        in_specs=in_specs,
        out_specs=out_specs,
        out_shape=out_shape,
        scratch_shapes=scratch,
        compiler_params=_params(("parallel", "arbitrary", "arbitrary")),
        name=name,
    )(*args)
    return outs if emit_stats else outs[0]


def _shifted(x, prev_row, next_row):
    ts = x.shape[0]
    rows = lax.broadcasted_iota(jnp.int32, x.shape, 0)
    xp = jnp.where(rows == 0, prev_row, pltpu.roll(x, 1, axis=0))
    xn = jnp.where(rows == ts - 1, next_row, pltpu.roll(x, ts - 1, axis=0))
    return xp, xn


def _tile_halo_rows(h, tm, seq):
    t, k = h.shape
    nt = t // tm
    h3 = h.reshape(nt, tm, k)
    zero = jnp.zeros((1, k), h.dtype)
    prev = jnp.concatenate([zero, h3[:-1, tm - 1]], axis=0)
    nxt = jnp.concatenate([h3[1:, 0], zero], axis=0)
    pos = jnp.arange(nt) % (seq // tm)
    prev = jnp.where((pos == 0)[:, None], jnp.zeros_like(prev), prev)
    nxt = jnp.where((pos == seq // tm - 1)[:, None], jnp.zeros_like(nxt), nxt)
    pad = jnp.zeros((nt, 6, k), h.dtype)
    return jnp.concatenate([prev[:, None], nxt[:, None], pad], axis=1).reshape(nt * 8, k)


def _proj_conv_kernel(x_ref, w_ref, halo_ref, cw_ref, ss_ref, *out_refs, nsplit, normalize, scale, head_dim):
    x = x_ref[...]
    o_ref = out_refs[0]
    w = o_ref.shape[1] // nsplit
    rs = _row_scale(ss_ref, x.shape[1], w)
    for s in range(nsplit):
        cs = slice(s * w, (s + 1) * w)
        pre = jnp.dot(x, w_ref[:, cs], preferred_element_type=f32) * rs
        pp, pn = _shifted(pre, halo_ref[0:1, cs], halo_ref[1:2, cs])
        y = pp * cw_ref[0:1, cs] + pre * cw_ref[1:2, cs] + pn * cw_ref[2:3, cs]
        y = y * jax.nn.sigmoid(y)
        for hh in range(w // head_dim):
            seg = y[:, hh * head_dim:(hh + 1) * head_dim]
            if normalize:
                ss = jnp.sum(seg * seg, axis=-1, keepdims=True)
                seg = seg * lax.rsqrt(ss + L2_EPS)
                if scale != 1.0:
                    seg = seg * scale
            hs = slice(s * w + hh * head_dim, s * w + (hh + 1) * head_dim)
            o_ref[:, hs] = seg.astype(o_ref.dtype)
            if len(out_refs) > 1:
                out_refs[1][hs, :] = seg.T.astype(out_refs[1].dtype)


def _proj_conv(h, ss, w, halo, conv_w, col_off, width, tm, head_dim, normalize, scale, transpose_out, tn=1024):
    t, k = h.shape
    tn = _pick(math.gcd(width, col_off) if col_off else width, tn, LANES)
    off = col_off // tn
    out_shape = [jax.ShapeDtypeStruct((t, width), bf16)]
    out_specs = [pl.BlockSpec((tm, tn), lambda i, j: (i, j))]
    if transpose_out:
        out_shape.append(jax.ShapeDtypeStruct((width, t), bf16))
        out_specs.append(pl.BlockSpec((tn, tm), lambda i, j: (j, i)))
    return pl.pallas_call(
        functools.partial(_proj_conv_kernel, nsplit=max(1, tn // 256), normalize=normalize, scale=scale,
                          head_dim=head_dim),
        grid=(t // tm, width // tn),
        in_specs=[pl.BlockSpec((tm, k), lambda i, j: (i, 0)),
                  pl.BlockSpec((k, tn), lambda i, j: (0, j + off)),
                  pl.BlockSpec((8, tn), lambda i, j: (i, j + off)),
                  pl.BlockSpec((3, tn), lambda i, j: (0, j)),
                  pl.BlockSpec((tm, LANES), lambda i, j: (i, 0))],
        out_specs=out_specs,
        out_shape=out_shape,
        compiler_params=_params(("parallel", "arbitrary")),
        name="proj_conv",
    )(h, w, halo, conv_w, ss)


def _gate_prep_kernel(gate_ref, prm_ref, gcol_ref, grow_ref, *, heads):
    c_len = DN_CHUNK
    row = lax.broadcasted_iota(jnp.int32, (c_len, c_len), 0)
    col = lax.broadcasted_iota(jnp.int32, (c_len, c_len), 1)
    tri_lo = (row >= col).astype(f32)
    tri_up = (row <= col).astype(f32)
    for c in range(gate_ref.shape[0] // c_len):
        rows = slice(c * c_len, (c + 1) * c_len)
        blk = gate_ref[rows, :]
        beta = jax.nn.sigmoid(blk)
        g = prm_ref[0:1, :] * jax.nn.softplus(blk + prm_ref[1:2, :])
        g_f = jnp.dot(tri_lo, g, preferred_element_type=f32, precision=lax.Precision.HIGHEST)
        g_b = jnp.dot(tri_up, g, preferred_element_type=f32, precision=lax.Precision.HIGHEST)
        out = jnp.where(col < 2 * heads, beta, jnp.where(col < 3 * heads, g_f, jnp.where(col < 4 * heads, g_b, 0.0)))
        gcol_ref[rows, :] = out
        grow_ref[c] = out.T


def _gate_prep(gates, prm, heads):
    t = gates.shape[0]
    n = t // DN_CHUNK
    cb = _pick(n, 8, 1)
    return pl.pallas_call(
        functools.partial(_gate_prep_kernel, heads=heads),
        grid=(n // cb,),
        in_specs=[pl.BlockSpec((cb * DN_CHUNK, LANES), lambda i: (i, 0)), pl.BlockSpec((2, LANES), lambda i: (0, 0))],
        out_specs=[pl.BlockSpec((cb * DN_CHUNK, LANES), lambda i: (i, 0)),
                   pl.BlockSpec((cb, LANES, DN_CHUNK), lambda i: (i, 0, 0))],
        out_shape=[jax.ShapeDtypeStruct((t, LANES), f32), jax.ShapeDtypeStruct((n, LANES, DN_CHUNK), f32)],
        compiler_params=_params(("parallel",)),
        name="gate_prep",
    )(gates, prm)


_M_NEG_LO, _M_NEG_UP, _M_OFFDIAG, _M_EYE, _M_LEAF, _M_OFF0 = 0, 1, 2, 3, 4, 5


def _dn_masks():
    c = DN_CHUNK
    r = np.arange(c)[:, None]
    s = np.arange(c)[None, :]
    ms = [np.where(r >= s, 0.0, -np.inf), np.where(r <= s, 0.0, -np.inf), (r != s) * 1.0, (r == s) * 1.0,
          (r // LEAF == s // LEAF) * 1.0]
    blk = LEAF
    while blk < c:
        ms.append(((r // (2 * blk) == s // (2 * blk)) & (r // blk != s // blk)) * 1.0)
        blk *= 2
    return jnp.asarray(np.stack(ms), f32)


def _block_rows(x, blk, parity):
    return jnp.concatenate([x[i * blk:(i + 1) * blk] for i in range(x.shape[0] // blk) if i % 2 == parity], axis=0)


def _merge_block_rows(base, rows, blk, parity, combine):
    out = []
    for i in range(base.shape[0] // blk):
        piece = base[i * blk:(i + 1) * blk]
        if i % 2 == parity:
            piece = combine(piece, rows[(i // 2) * blk:(i // 2 + 1) * blk])
        out.append(piece)
    return jnp.concatenate(out, axis=0)


def _tri_inverse(a_mats, lower, m_ref):
    n = a_mats[0].shape[1]
    leaf, eye = m_ref[_M_LEAF], m_ref[_M_EYE]
    ps = [-(a * leaf) for a in a_mats]
    xs = [eye + p for p in ps]
    ps = [_bdot(p, p) for p in ps]
    for _ in range(int(math.log2(LEAF)) - 2):
        pbs = [p.astype(bf16) for p in ps]
        rs = [jnp.dot(pb, jnp.concatenate([pb, x.astype(bf16)], axis=1), preferred_element_type=f32)
              for pb, x in zip(pbs, xs)]
        ps = [r[:, :n] for r in rs]
        xs = [x + r[:, n:] for x, r in zip(xs, rs)]
    ts = [x + _bdot(p, x) for p, x in zip(ps, xs)]
    parity = [1 if lo else 0 for lo in lower]
    blk = LEAF
    for lvl in range(m_ref.shape[0] - _M_OFF0):
        off = m_ref[_M_OFF0 + lvl]
        zeros = jnp.zeros((n, n), bf16)
        vs = [_bdot(_block_rows(a * off, blk, pr), t).astype(bf16) for a, t, pr in zip(a_mats, ts, parity)]
        v_full = [_merge_block_rows(zeros, v, blk, pr, lambda _, r: r) for v, pr in zip(vs, parity)]
        ds = [_bdot(_block_rows(t, blk, pr), vf) for t, vf, pr in zip(ts, v_full, parity)]
        ts = [_merge_block_rows(t, d, blk, pr, lambda p, r: p - r) for t, d, pr in zip(ts, ds, parity)]
        blk *= 2
    return ts


def _dn_kernel(qf_ref, kf_ref, vf_ref, ktf_ref, gcf_ref, grf_ref,
               qb_ref, kb_ref, vb_ref, ktb_ref, gcb_ref, grb_ref, m_ref,
               of_ref, ob_ref, st_s, *, heads, head_block, head_dim):
    c_len = DN_CHUNK
    hb = pl.program_id(1)

    @pl.when(pl.program_id(2) == 0)
    def _():
        st_s[...] = jnp.zeros_like(st_s)

    lane = lax.broadcasted_iota(jnp.int32, (c_len, LANES), 1)
    dirs = ((qf_ref, kf_ref, vf_ref, ktf_ref, gcf_ref, grf_ref, of_ref, _M_NEG_LO, c_len - 1),
            (qb_ref, kb_ref, vb_ref, ktb_ref, gcb_ref, grb_ref, ob_ref, _M_NEG_UP, 0))
    chains = [(d, hh) for d in range(2) for hh in range(head_block)]
    sls = [slice(hh * head_dim, (hh + 1) * head_dim) for _, hh in chains]

    q_c = [dirs[d][0][:, sl] for (d, _), sl in zip(chains, sls)]
    k_c = [dirs[d][1][:, sl] for (d, _), sl in zip(chains, sls)]
    kt_c = [dirs[d][3][sl, :] for (d, _), sl in zip(chains, sls)]
    states = [st_s[d, hh] for d, hh in chains]
    gram = [jnp.dot(jnp.concatenate([q, k], axis=0), kt, preferred_element_type=f32)
            for q, k, kt in zip(q_c, k_c, kt_c)]
    kqs = [jnp.dot(jnp.concatenate([k, q], axis=0), s.astype(bf16), preferred_element_type=f32)
           for q, k, s in zip(q_c, k_c, states)]

    gc, gr, br = [], [], []
    for d, hh in chains:
        head = hb * head_block + hh
        g_col = (2 + d) * heads + head
        gc.append(jnp.sum(jnp.where(lane == g_col, dirs[d][4][...], 0.0), axis=-1, keepdims=True))
        gr.append(dirs[d][5][0, pl.ds(g_col, 1), :])
        br.append(dirs[d][5][0, pl.ds(d * heads + head, 1), :])

    e = [jnp.exp(c - r + m_ref[dirs[d][7]]) * b for (d, _), c, r, b in zip(chains, gc, gr, br)]
    attn = [(g[:c_len] * ee).astype(bf16) for g, ee in zip(gram, e)]
    offdiag = m_ref[_M_OFFDIAG]
    tinv = _tri_inverse([g[c_len:] * ee * offdiag for g, ee in zip(gram, e)], [d == 0 for d, _ in chains], m_ref)

    eg = [jnp.exp(c) for c in gc]
    r = [dirs[d][2][:, sl].astype(f32) - ee * ks[:c_len] for (d, _), sl, ee, ks in zip(chains, sls, eg, kqs)]
    y = [jnp.dot(t.astype(bf16), rr.astype(bf16), preferred_element_type=f32).astype(bf16)
         for t, rr in zip(tinv, r)]
    g_end = [jnp.sum(jnp.where(lane[0:1, :] == dirs[d][8], r_, 0.0), axis=-1, keepdims=True)
             for (d, _), r_ in zip(chains, gr)]
    kt_dec = [(kt.astype(f32) * (jnp.exp(ge - r_) * b)).astype(bf16)
              for kt, ge, r_, b in zip(kt_c, g_end, gr, br)]
    ay = [jnp.dot(jnp.concatenate([a, kd], axis=0), yy, preferred_element_type=f32)
          for a, kd, yy in zip(attn, kt_dec, y)]
    for (d, hh), sl, ee, ks, ay_, ge, s in zip(chains, sls, eg, kqs, ay, g_end, states):
        dirs[d][6][:, sl] = (ee * ks[c_len:] + ay_[:c_len]).astype(dirs[d][6].dtype)
        st_s[d, hh] = jnp.exp(ge) * s + ay_[c_len:]


def _deltanet(q, k, v, kt, gcol, grow, seq, heads, head_dim):
    t = q.shape[0]
    nb = t // seq
    n = seq // DN_CHUNK
    hblk = _pick(heads, DN_HEAD_BLOCK, 1)
    w = hblk * head_dim
    fwd = lambda b, h, j: b * n + j
    bwd = lambda b, h, j: b * n + (n - 1 - j)

    def specs(pos):
        tok = pl.BlockSpec((DN_CHUNK, w), lambda b, h, j: (pos(b, h, j), h))
        return [tok, tok, tok,
                pl.BlockSpec((w, DN_CHUNK), lambda b, h, j: (h, pos(b, h, j))),
                pl.BlockSpec((DN_CHUNK, LANES), lambda b, h, j: (pos(b, h, j), 0)),
                pl.BlockSpec((1, LANES, DN_CHUNK), lambda b, h, j: (pos(b, h, j), 0, 0))]

    masks = _dn_masks()
    out_sds = jax.ShapeDtypeStruct((t, heads * head_dim), bf16)
    return pl.pallas_call(
        functools.partial(_dn_kernel, heads=heads, head_block=hblk, head_dim=head_dim),
        grid=(nb, heads // hblk, n),
        in_specs=specs(fwd) + specs(bwd) + [pl.BlockSpec(masks.shape, lambda b, h, j: (0, 0, 0))],
        out_specs=[pl.BlockSpec((DN_CHUNK, w), lambda b, h, j: (fwd(b, h, j), h)),
                   pl.BlockSpec((DN_CHUNK, w), lambda b, h, j: (bwd(b, h, j), h))],
        out_shape=[out_sds, out_sds],
        scratch_shapes=[pltpu.VMEM((2, hblk, head_dim, head_dim), f32)],
        compiler_params=_params(("parallel", "parallel", "arbitrary")),
        name="deltanet",
    )(q, k, v, kt, gcol, grow, q, k, v, kt, gcol, grow, masks)


def _mix_kernel(of_ref, ob_ref, z_ref, dng_ref, u_ref, v_ref, ws_ref, bias_ref, sgg_ref, o_ref, *,
                dn_heads, dn_dim, sg_heads, sg_dim, sg_chunk):
    dnw = dn_heads * dn_dim
    for hh in range(dn_heads):
        sl = slice(hh * dn_dim, (hh + 1) * dn_dim)
        o = of_ref[:, sl].astype(f32) + ob_ref[:, sl].astype(f32)
        ms = jnp.mean(o * o, axis=-1, keepdims=True)
        z = z_ref[:, sl].astype(f32)
        o_ref[:, sl] = (o * lax.rsqrt(ms + NORM_EPS) * dng_ref[...] * (z * jax.nn.sigmoid(z))).astype(o_ref.dtype)
    for c in range(o_ref.shape[0] // sg_chunk):
        rows = slice(c * sg_chunk, (c + 1) * sg_chunk)
        for g in range(sg_heads):
            sl = slice(g * sg_dim, (g + 1) * sg_dim)
            v = jax.nn.gelu(v_ref[rows, sl].astype(f32))
            ms = jnp.mean(v * v, axis=-1, keepdims=True)
            vn = v * lax.rsqrt(ms + NORM_EPS) * sgg_ref[:, sl]
            mixed = jnp.dot(ws_ref[g], vn.astype(bf16), preferred_element_type=f32) + bias_ref[g]
            u = jax.nn.gelu(u_ref[rows, sl].astype(f32))
            o_ref[rows, dnw + g * sg_dim:dnw + (g + 1) * sg_dim] = (u * mixed).astype(o_ref.dtype)


def _mixer_out(o_f, o_b, z, dn_norm_g, dn_heads, p_sg, sgw, ws, bias, sg_norm_g, sg_heads, sg_chunk):
    t, dnw = o_f.shape
    tb = _pick(t, 2 * sg_chunk, sg_chunk)
    sg_dim = sgw // sg_heads
    return pl.pallas_call(
        functools.partial(_mix_kernel, dn_heads=dn_heads, dn_dim=dnw // dn_heads, sg_heads=sg_heads,
                          sg_dim=sg_dim, sg_chunk=sg_chunk),
        grid=(t // tb,),
        in_specs=[pl.BlockSpec((tb, dnw), lambda i: (i, 0)), pl.BlockSpec((tb, dnw), lambda i: (i, 0)),
                  pl.BlockSpec((tb, dnw), lambda i: (i, 0)),
                  pl.BlockSpec((1, dnw // dn_heads), lambda i: (0, 0)),
                  pl.BlockSpec((tb, sgw), lambda i: (i, 0)), pl.BlockSpec((tb, sgw), lambda i: (i, 1)),
                  pl.BlockSpec((sg_heads, sg_chunk, sg_chunk), lambda i: (0, 0, 0)),
                  pl.BlockSpec((sg_heads, sg_chunk, sg_dim), lambda i: (0, 0, 0)),
                  pl.BlockSpec((1, sgw), lambda i: (0, 0))],
        out_specs=pl.BlockSpec((tb, dnw + sgw), lambda i: (i, 0)),
        out_shape=jax.ShapeDtypeStruct((t, dnw + sgw), bf16),
        compiler_params=_params(("parallel",)),
        name="mixer_out",
    )(o_f, o_b, z, dn_norm_g, p_sg, p_sg, ws, bias, sg_norm_g)


def _up_kernel(x_ref, wg_ref, wu_ref, halo_ref, cw_ref, ss_ref, o_ref, xn_s, *, nsplit, row_chunk):
    @pl.when(pl.program_id(1) == 0)
    def _():
        for c in range(x_ref.shape[0] // row_chunk):
            rows = slice(c * row_chunk, (c + 1) * row_chunk)
            r = lax.rsqrt(ss_ref[rows, :] * (1.0 / x_ref.shape[1]) + NORM_EPS)
            r = jnp.concatenate([r] * (x_ref.shape[1] // LANES), axis=1)
            xn_s[rows, :] = (x_ref[rows, :].astype(f32) * r).astype(xn_s.dtype)

    x = xn_s[...]
    w = o_ref.shape[1] // nsplit
    for s in range(nsplit):
        cs = slice(s * w, (s + 1) * w)
        g = jnp.dot(x, wg_ref[:, cs], preferred_element_type=f32)
        u = jnp.dot(x, wu_ref[:, cs], preferred_element_type=f32)
        gp, gn = _shifted(g, halo_ref[0:1, cs], halo_ref[1:2, cs])
        y = gp * cw_ref[0:1, cs] + g * cw_ref[1:2, cs] + gn * cw_ref[2:3, cs]
        o_ref[:, cs] = (y * jax.nn.sigmoid(y) * u).astype(o_ref.dtype)


def _up_conv_gate(h, ss, w_up, conv_w, ffp, seq, tm=1024, tn=512):
    t, k = h.shape
    tm = _pick(seq, tm, BF16_SUBLANES)
    tn = _pick(ffp, tn, LANES)
    halo = _matmul(_tile_halo_rows(h, tm, seq), w_up, f32, scale=_tile_halo_rows(ss, tm, seq), n_cols=ffp,
                   name="up_halo")
    uoff = ffp // tn
    return pl.pallas_call(
        functools.partial(_up_kernel, nsplit=max(1, tn // 256), row_chunk=_pick(tm, 128, BF16_SUBLANES)),
        grid=(t // tm, ffp // tn),
        in_specs=[pl.BlockSpec((tm, k), lambda i, j: (i, 0)),
                  pl.BlockSpec((k, tn), lambda i, j: (0, j)),
                  pl.BlockSpec((k, tn), lambda i, j: (0, j + uoff)),
                  pl.BlockSpec((8, tn), lambda i, j: (i, j)),
                  pl.BlockSpec((3, tn), lambda i, j: (0, j)),
                  pl.BlockSpec((tm, LANES), lambda i, j: (i, 0))],
        out_specs=pl.BlockSpec((tm, tn), lambda i, j: (i, j)),
        out_shape=jax.ShapeDtypeStruct((t, ffp), bf16),
        scratch_shapes=[pltpu.VMEM((tm, k), bf16)],
        compiler_params=_params(("parallel", "arbitrary")),
        name="up_conv_gate",
    )(h, w_up, w_up, halo, conv_w, ss)


def _cast_kernel(x_ref, g_ref, o_ref, *, zero_blocks, axis):
    blk = pl.program_id(axis)
    is_zero = functools.reduce(jnp.logical_or, [blk == z for z in zero_blocks], False)
    x = x_ref[...]
    if g_ref is not None:
        x = x * jnp.concatenate([g_ref[...]] * (x.shape[1] // LANES), axis=1)
    o_ref[...] = jnp.where(is_zero, 0.0, x).astype(o_ref.dtype)


def _cast_pad(w, layer, axis, segments, pad_to=None, row_gain=None):
    _, rows, cols = w.shape
    blk = functools.reduce(math.gcd, [v for seg in segments for v in seg if v] + [1024])
    src_of, zero_blocks = [], []
    for start, length in segments:
        padded = _round_up(length, pad_to) if pad_to else length
        for b in range(padded // blk):
            if b * blk < length:
                src_of.append((start + b * blk) // blk)
            else:
                zero_blocks.append(len(src_of))
                src_of.append(src_of[-1])
    nblk = len(src_of)
    table = jnp.asarray(src_of, jnp.int32)
    if axis == 1:
        tr = _pick(rows, CAST_BLOCK_ELEMS // blk, 8)
        grid = (rows // tr, nblk)
        in_spec = pl.BlockSpec((None, tr, blk), lambda i, j, tab: (layer, i, tab[j]))
        out_spec = pl.BlockSpec((tr, blk), lambda i, j, tab: (i, j))
        out_shape = (rows, nblk * blk)
    else:
        grid = (1, nblk)
        in_spec = pl.BlockSpec((None, blk, cols), lambda i, j, tab: (layer, tab[j], 0))
        out_spec = pl.BlockSpec((blk, cols), lambda i, j, tab: (j, 0))
        out_shape = (nblk * blk, cols)

    in_specs, args = [in_spec], [table, w]
    if row_gain is not None:
        assert axis == 1
        in_specs.append(pl.BlockSpec((tr, LANES), lambda i, j, tab: (i, 0)))
        args.append(jnp.broadcast_to(row_gain.astype(f32)[:, None], (rows, LANES)))

    def body(tab_ref, x_ref, *rest):
        _cast_kernel(x_ref, rest[0] if row_gain is not None else None, rest[-1], zero_blocks=zero_blocks, axis=1)

    return pl.pallas_call(
        body,
        grid_spec=pltpu.PrefetchScalarGridSpec(num_scalar_prefetch=1, grid=grid, in_specs=in_specs,
                                               out_specs=out_spec),
        out_shape=jax.ShapeDtypeStruct(out_shape, bf16),
        compiler_params=_params(("parallel", "parallel")),
        name="cast_pad",
    )(*args)


def _cast_transpose_kernel(a_ref, b_ref, g_ref, o_ref):
    x = jnp.concatenate([a_ref[...], b_ref[...]], axis=0) * g_ref[...]
    o_ref[...] = x.T.astype(o_ref.dtype)


def _cast_transpose(w_t, layer, row_start, n_rows, col_gain):
    half = LANES // 2
    assert row_start % half == 0 and n_rows % LANES == 0
    k = w_t.shape[2]
    base = row_start // half
    piece = lambda d: pl.BlockSpec((None, half, k), lambda j: (layer, base + 2 * j + d, 0))
    return pl.pallas_call(
        _cast_transpose_kernel,
        grid=(n_rows // LANES,),
        in_specs=[piece(0), piece(1), pl.BlockSpec((1, k), lambda j: (0, 0))],
        out_specs=pl.BlockSpec((k, LANES), lambda j: (0, j)),
        out_shape=jax.ShapeDtypeStruct((k, n_rows), bf16),
        compiler_params=_params(("parallel",)),
        name="cast_transpose",
    )(w_t, w_t, col_gain.astype(f32).reshape(1, k))


def _prepare_weights(norm1_g, w_in, qkv_conv_w, a_log, dt_bias, dn_norm_g, sg_norm_g, sg_w, sg_b,
                     w_out, norm2_g, w_up, ffn_conv_w, w_down):
    depth = w_in.shape[0]
    heads = a_log.shape[-1]
    dnw = qkv_conv_w.shape[-1] // 3
    sgw = sg_norm_g.shape[-1]
    ff = ffn_conv_w.shape[-1]
    ffp = _round_up(ff, 1024)
    pad_f = ffp - ff
    g0 = 4 * dnw
    g1 = g0 + 4 * heads
    assert 4 * heads <= LANES
    sg_heads, sg_chunk = sg_w.shape[1], sg_w.shape[2]
    dims = dict(heads=heads, dnw=dnw, sgw=sgw, ffp=ffp, sg_heads=sg_heads, sg_chunk=sg_chunk)
    lead = jnp.zeros((2 * heads,), f32)
    tail = jnp.zeros((LANES - 4 * heads,), f32)
    layers = []
    w_in_t = jnp.swapaxes(w_in, 1, 2)
    for l in range(depth):
        neg_a = -jnp.exp(a_log[l].astype(f32)).reshape(2 * heads)
        dtb = dt_bias[l].astype(f32).reshape(2 * heads)
        layers.append(dict(
            w_qkvz=_cast_transpose(w_in_t, l, 0, g0, norm1_g[l]),
            w_sg=_cast_transpose(w_in_t, l, g1, 2 * sgw, norm1_g[l]),
            w_gate=_cast_transpose(w_in_t, l, g0, LANES, norm1_g[l]),
            qkv_cw=qkv_conv_w[l].astype(f32),
            prm=jnp.stack([jnp.concatenate([lead, neg_a, tail]), jnp.concatenate([lead, dtb, tail])]),
            dn_norm_g=dn_norm_g[l].astype(f32).reshape(1, -1),
            sg_norm_g=sg_norm_g[l].astype(f32).reshape(1, -1),
            sg_w=sg_w[l].astype(bf16),
            sg_bias=jnp.broadcast_to(sg_b[l].astype(f32)[..., None], (sg_heads, sg_chunk, sgw // sg_heads)),
            w_out=_cast_pad(w_out, l, 1, [(0, w_out.shape[2])]),
            w_up=_cast_pad(w_up, l, 1, [(0, ff), (ff, ff)], pad_to=1024, row_gain=norm2_g[l]),
            ffn_cw=jnp.pad(ffn_conv_w[l].astype(f32), ((0, 0), (0, pad_f))),
            w_down=_cast_pad(w_down, l, 0, [(0, ff)], pad_to=1024),
        ))
    return layers, dims


def _trunk(x, layers, dims, final_norm_g):
    nb, seq, d = x.shape
    heads, dnw, sgw, ffp = dims["heads"], dims["dnw"], dims["sgw"], dims["ffp"]
    head_dim = dnw // heads
    xt = x.reshape(nb * seq, d)
    xb, ss = _rowstats(xt)
    tm = _pick(seq, 1024, BF16_SUBLANES)
    for w in layers:
        gates = _matmul(xb, w["w_gate"], f32, scale=ss, name="gate_proj")
        halo = _matmul(_tile_halo_rows(xb, tm, seq), w["w_qkvz"], f32, scale=_tile_halo_rows(ss, tm, seq),
                       n_cols=3 * dnw, name="qkv_halo")
        cw = w["qkv_cw"]
        (q,) = _proj_conv(xb, ss, w["w_qkvz"], halo, cw[:, :dnw], 0, dnw, tm, head_dim, True, head_dim ** -0.5,
                          False)
        k, kt = _proj_conv(xb, ss, w["w_qkvz"], halo, cw[:, dnw:2 * dnw], dnw, dnw, tm, head_dim, True, 1.0, True)
        (v,) = _proj_conv(xb, ss, w["w_qkvz"], halo, cw[:, 2 * dnw:], 2 * dnw, dnw, tm, head_dim, False, 1.0,
                          False)
        z = _matmul(xb, w["w_qkvz"], bf16, scale=ss, col_off=3 * dnw, name="z_proj")
        p_sg = _matmul(xb, w["w_sg"], bf16, scale=ss, name="sg_proj")
        gcol, grow = _gate_prep(gates, w["prm"], heads)
        o_f, o_b = _deltanet(q, k, v, kt, gcol, grow, seq, heads, head_dim)
        o = _mixer_out(o_f, o_b, z, w["dn_norm_g"], heads, p_sg, sgw,
                       w["sg_w"], w["sg_bias"], w["sg_norm_g"], dims["sg_heads"], dims["sg_chunk"])
        xt, xb, ss = _matmul(o, w["w_out"], f32, res=xt, emit_stats=True, tn=512, name="out_proj")
        hid = _up_conv_gate(xb, ss, w["w_up"], w["ffn_cw"], ffp, seq)
        if w is layers[-1]:
            xt = _matmul(hid, w["w_down"], f32, res=xt, tk=2816, name="down_proj")
        else:
            xt, xb, ss = _matmul(hid, w["w_down"], f32, res=xt, emit_stats=True, tk=2816, name="down_proj")
    return _rmsnorm(xt, final_norm_g, f32).reshape(nb, seq, d)


def kernel(x_prompt, x_sample, norm1_g, w_in, qkv_conv_w, a_log, dt_bias, dn_norm_g, sg_norm_g, sg_w, sg_b, w_out, norm2_g, w_up, ffn_conv_w, w_down, final_norm_g):
    layers, dims = _prepare_weights(norm1_g, w_in, qkv_conv_w, a_log, dt_bias, dn_norm_g, sg_norm_g, sg_w, sg_b,
                                    w_out, norm2_g, w_up, ffn_conv_w, w_down)
    return (_trunk(x_prompt, layers, dims, final_norm_g), _trunk(x_sample, layers, dims, final_norm_g))
```

```python
import functools
import math

import jax
import jax.numpy as jnp
import numpy as np
from jax import lax
from jax.experimental import pallas as pl
from jax.experimental.pallas import tpu as pltpu

NORM_EPS = 1e-6
L2_EPS = 1e-6
LANES = 128
BF16_SUBLANES = 16
DN_CHUNK = 128
LEAF = 16
DN_HEAD_BLOCK = 16
VMEM_LIMIT = 56 * 1024 * 1024
CAST_BLOCK_ELEMS = 1024 * 1024

f32 = jnp.float32
bf16 = jnp.bfloat16


def _pick(n, target, mult):
    best = None
    d = mult
    while d <= min(n, target):
        if n % d == 0:
            best = d
        d += mult
    return best if best is not None else n


def _round_up(n, m):
    return (n + m - 1) // m * m


def _params(sem):
    return pltpu.CompilerParams(dimension_semantics=sem, vmem_limit_bytes=VMEM_LIMIT)


def _bdot(a, b):
    return jnp.dot(a.astype(bf16), b.astype(bf16), preferred_element_type=f32)


def _rmsnorm_kernel(x_ref, g_ref, o_ref):
    x = x_ref[...]
    ms = jnp.mean(x * x, axis=-1, keepdims=True)
    o_ref[...] = (x * lax.rsqrt(ms + NORM_EPS) * g_ref[...]).astype(o_ref.dtype)


def _rmsnorm(x, g, out_dtype):
    t, d = x.shape
    tm = _pick(t, 256, 8)
    return pl.pallas_call(
        _rmsnorm_kernel,
        grid=(t // tm,),
        in_specs=[pl.BlockSpec((tm, d), lambda i: (i, 0)), pl.BlockSpec((1, d), lambda i: (0, 0))],
        out_specs=pl.BlockSpec((tm, d), lambda i: (i, 0)),
        out_shape=jax.ShapeDtypeStruct((t, d), out_dtype),
        compiler_params=_params(("parallel",)),
        name="rmsnorm",
    )(x, g.reshape(1, d).astype(f32))


def _rowstats_kernel(x_ref, xb_ref, ss_ref):
    x = x_ref[...]
    xb_ref[...] = x.astype(xb_ref.dtype)
    ss_ref[...] = jnp.broadcast_to(jnp.sum(x * x, axis=-1, keepdims=True), ss_ref.shape)


def _rowstats(x):
    t, d = x.shape
    tm = _pick(t, 256, 8)
    return pl.pallas_call(
        _rowstats_kernel,
        grid=(t // tm,),
        in_specs=[pl.BlockSpec((tm, d), lambda i: (i, 0))],
        out_specs=[pl.BlockSpec((tm, d), lambda i: (i, 0)), pl.BlockSpec((tm, LANES), lambda i: (i, 0))],
        out_shape=[jax.ShapeDtypeStruct((t, d), bf16), jax.ShapeDtypeStruct((t, LANES), f32)],
        compiler_params=_params(("parallel",)),
        name="rowstats",
    )(x)


def _row_scale(ss_ref, d, width):
    r = lax.rsqrt(ss_ref[...] * (1.0 / d) + NORM_EPS)
    return r if width == LANES else jnp.concatenate([r] * (width // LANES), axis=1)


def _act_tail(x, act, gain, head_dim):
    if act == "silu":
        return x * jax.nn.sigmoid(x)
    x = jax.nn.gelu(x)
    if act == "gelu":
        return x
    heads = []
    for hh in range(x.shape[1] // head_dim):
        seg = x[:, hh * head_dim:(hh + 1) * head_dim]
        ms = jnp.mean(seg * seg, axis=-1, keepdims=True)
        heads.append(seg * lax.rsqrt(ms + NORM_EPS) * gain[:, hh * head_dim:(hh + 1) * head_dim])
    return jnp.concatenate(heads, axis=1)


def _mm_kernel(*refs, nk, has_res, has_scale, emit_stats, d_norm, act, head_dim):
    refs = list(refs)
    a_ref, w_ref = refs[:2]
    pos = 2
    r_ref = s_ref = g_ref = None
    if has_res:
        r_ref = refs[pos]
        pos += 1
    if has_scale:
        s_ref = refs[pos]
        pos += 1
    if act == "gelu_norm":
        g_ref = refs[pos]
        pos += 1
    o_ref = refs[pos]
    if act is not None:
        a = a_ref[...]
        piece = 2 * LANES
        rs = _row_scale(s_ref, d_norm, piece)
        for c in range(o_ref.shape[1] // piece):
            cs = slice(c * piece, (c + 1) * piece)
            out = jnp.dot(a, w_ref[:, cs], preferred_element_type=f32) * rs
            gain = None if g_ref is None else g_ref[:, cs]
            o_ref[:, cs] = _act_tail(out, act, gain, head_dim).astype(o_ref.dtype)
        return
    pos += 1
    xb_ref = ss_ref = None
    if emit_stats:
        xb_ref, ss_ref = refs[pos:pos + 2]
        pos += 2
    scratch = refs[pos:]

    def finish(out):
        if s_ref is not None:
            out = out * _row_scale(s_ref, d_norm, out.shape[1])
        if r_ref is not None:
            out = out + r_ref[...]
        o_ref[...] = out.astype(o_ref.dtype)
        if emit_stats:
            xb_ref[...] = out.astype(xb_ref.dtype)
            part = jnp.broadcast_to(jnp.sum(out * out, axis=-1, keepdims=True), ss_ref.shape)
            j = pl.program_id(1)

            @pl.when(j == 0)
            def _():
                ss_ref[...] = part

            @pl.when(j > 0)
            def _():
                ss_ref[...] += part

    part = jnp.dot(a_ref[...], w_ref[...], preferred_element_type=f32)
    if nk == 1:
        finish(part)
        return
    acc_ref = scratch[0]
    k = pl.program_id(2)

    @pl.when(k == 0)
    def _():
        acc_ref[...] = part

    @pl.when(k > 0)
    def _():
        acc_ref[...] += part

    @pl.when(k == nk - 1)
    def _():
        finish(acc_ref[...])


def _matmul(a, w, out_dtype, res=None, scale=None, emit_stats=False, act=None, gain=None, head_dim=LANES,
            tm=1024, tn=1024, tk=4096, col_off=0, n_cols=None, name="matmul"):
    m, k = a.shape
    n = w.shape[1] - col_off if n_cols is None else n_cols
    tm = _pick(m, tm, 8)
    tn = _pick(math.gcd(n, col_off) if col_off else n, tn, LANES)
    tk = _pick(k, tk, LANES)
    nk = k // tk
    off = col_off // tn
    in_specs = [pl.BlockSpec((tm, tk), lambda i, j, kk: (i, kk)),
                pl.BlockSpec((tk, tn), lambda i, j, kk: (kk, j + off))]
    args = [a, w]
    if res is not None:
        in_specs.append(pl.BlockSpec((tm, tn), lambda i, j, kk: (i, j)))
        args.append(res)
    if scale is not None:
        in_specs.append(pl.BlockSpec((tm, LANES), lambda i, j, kk: (i, 0)))
        args.append(scale)
    if act is not None:
        assert nk == 1 and res is None and scale is not None and not emit_stats and tn % (2 * LANES) == 0
    if act == "gelu_norm":
        in_specs.append(pl.BlockSpec((1, tn), lambda i, j, kk: (0, j)))
        args.append(gain)
    out_specs = [pl.BlockSpec((tm, tn), lambda i, j, kk: (i, j))]
    out_shape = [jax.ShapeDtypeStruct((m, n), out_dtype)]
    if emit_stats:
        out_specs += [pl.BlockSpec((tm, tn), lambda i, j, kk: (i, j)), pl.BlockSpec((tm, LANES), lambda i, j, kk: (i, 0))]
        out_shape += [jax.ShapeDtypeStruct((m, n), bf16), jax.ShapeDtypeStruct((m, LANES), f32)]
    scratch = [pltpu.VMEM((tm, tn), f32)] if nk > 1 else []
    outs = pl.pallas_call(
        functools.partial(_mm_kernel, nk=nk, has_res=res is not None, has_scale=scale is not None,
                          emit_stats=emit_stats, d_norm=k, act=act, head_dim=head_dim),
        grid=(m // tm, n // tn, nk),
        in_specs=in_specs,
        out_specs=out_specs,
        out_shape=out_shape,
        scratch_shapes=scratch,
        compiler_params=_params(("parallel", "arbitrary", "arbitrary")),
        name=name,
    )(*args)
    return outs if emit_stats else outs[0]


def _shifted(x, prev_row, next_row):
    ts = x.shape[0]
    rows = lax.broadcasted_iota(jnp.int32, x.shape, 0)
    xp = jnp.where(rows == 0, prev_row, pltpu.roll(x, 1, axis=0))
    xn = jnp.where(rows == ts - 1, next_row, pltpu.roll(x, ts - 1, axis=0))
    return xp, xn


def _tile_halo_rows(h, tm, seq):
    t, k = h.shape
    nt = t // tm
    h3 = h.reshape(nt, tm, k)
    zero = jnp.zeros((1, k), h.dtype)
    prev = jnp.concatenate([zero, h3[:-1, tm - 1]], axis=0)
    nxt = jnp.concatenate([h3[1:, 0], zero], axis=0)
    pos = jnp.arange(nt) % (seq // tm)
    prev = jnp.where((pos == 0)[:, None], jnp.zeros_like(prev), prev)
    nxt = jnp.where((pos == seq // tm - 1)[:, None], jnp.zeros_like(nxt), nxt)
    pad = jnp.zeros((nt, 6, k), h.dtype)
    return jnp.concatenate([prev[:, None], nxt[:, None], pad], axis=1).reshape(nt * 8, k)


def _proj_conv_kernel(x_ref, w_ref, halo_ref, cw_ref, ss_ref, *out_refs, nsplit, normalize, scale, head_dim):
    x = x_ref[...]
    o_ref = out_refs[0]
    w = o_ref.shape[1] // nsplit
    rs = _row_scale(ss_ref, x.shape[1], w)
    for s in range(nsplit):
        cs = slice(s * w, (s + 1) * w)
        pre = jnp.dot(x, w_ref[:, cs], preferred_element_type=f32) * rs
        pp, pn = _shifted(pre, halo_ref[0:1, cs], halo_ref[1:2, cs])
        y = pp * cw_ref[0:1, cs] + pre * cw_ref[1:2, cs] + pn * cw_ref[2:3, cs]
        y = y * jax.nn.sigmoid(y)
        for hh in range(w // head_dim):
            seg = y[:, hh * head_dim:(hh + 1) * head_dim]
            if normalize:
                ss = jnp.sum(seg * seg, axis=-1, keepdims=True)
                seg = seg * lax.rsqrt(ss + L2_EPS)
                if scale != 1.0:
                    seg = seg * scale
            hs = slice(s * w + hh * head_dim, s * w + (hh + 1) * head_dim)
            o_ref[:, hs] = seg.astype(o_ref.dtype)
            if len(out_refs) > 1:
                out_refs[1][hs, :] = seg.T.astype(out_refs[1].dtype)


def _proj_conv(h, ss, w, halo, conv_w, col_off, width, tm, head_dim, normalize, scale, transpose_out, tn=1024):
    t, k = h.shape
    tn = _pick(math.gcd(width, col_off) if col_off else width, tn, LANES)
    off = col_off // tn
    out_shape = [jax.ShapeDtypeStruct((t, width), bf16)]
    out_specs = [pl.BlockSpec((tm, tn), lambda i, j: (i, j))]
    if transpose_out:
        out_shape.append(jax.ShapeDtypeStruct((width, t), bf16))
        out_specs.append(pl.BlockSpec((tn, tm), lambda i, j: (j, i)))
    return pl.pallas_call(
        functools.partial(_proj_conv_kernel, nsplit=max(1, tn // 256), normalize=normalize, scale=scale,
                          head_dim=head_dim),
        grid=(t // tm, width // tn),
        in_specs=[pl.BlockSpec((tm, k), lambda i, j: (i, 0)),
                  pl.BlockSpec((k, tn), lambda i, j: (0, j + off)),
                  pl.BlockSpec((8, tn), lambda i, j: (i, j + off)),
                  pl.BlockSpec((3, tn), lambda i, j: (0, j)),
                  pl.BlockSpec((tm, LANES), lambda i, j: (i, 0))],
        out_specs=out_specs,
        out_shape=out_shape,
        compiler_params=_params(("parallel", "arbitrary")),
        name="proj_conv",
    )(h, w, halo, conv_w, ss)


def _gate_prep_kernel(gate_ref, prm_ref, gcol_ref, grow_ref, *, heads):
    c_len = DN_CHUNK
    row = lax.broadcasted_iota(jnp.int32, (c_len, c_len), 0)
    col = lax.broadcasted_iota(jnp.int32, (c_len, c_len), 1)
    tri_lo = (row >= col).astype(f32)
    tri_up = (row <= col).astype(f32)
    for c in range(gate_ref.shape[0] // c_len):
        rows = slice(c * c_len, (c + 1) * c_len)
        blk = gate_ref[rows, :]
        beta = jax.nn.sigmoid(blk)
        g = prm_ref[0:1, :] * jax.nn.softplus(blk + prm_ref[1:2, :])
        g_f = jnp.dot(tri_lo, g, preferred_element_type=f32, precision=lax.Precision.HIGHEST)
        g_b = jnp.dot(tri_up, g, preferred_element_type=f32, precision=lax.Precision.HIGHEST)
        out = jnp.where(col < 2 * heads, beta, jnp.where(col < 3 * heads, g_f, jnp.where(col < 4 * heads, g_b, 0.0)))
        gcol_ref[rows, :] = out
        grow_ref[c] = out.T


def _gate_prep(gates, prm, heads):
    t = gates.shape[0]
    n = t // DN_CHUNK
    cb = _pick(n, 8, 1)
    return pl.pallas_call(
        functools.partial(_gate_prep_kernel, heads=heads),
        grid=(n // cb,),
        in_specs=[pl.BlockSpec((cb * DN_CHUNK, LANES), lambda i: (i, 0)), pl.BlockSpec((2, LANES), lambda i: (0, 0))],
        out_specs=[pl.BlockSpec((cb * DN_CHUNK, LANES), lambda i: (i, 0)),
                   pl.BlockSpec((cb, LANES, DN_CHUNK), lambda i: (i, 0, 0))],
        out_shape=[jax.ShapeDtypeStruct((t, LANES), f32), jax.ShapeDtypeStruct((n, LANES, DN_CHUNK), f32)],
        compiler_params=_params(("parallel",)),
        name="gate_prep",
    )(gates, prm)


_M_NEG_LO, _M_NEG_UP, _M_OFFDIAG, _M_EYE, _M_LEAF, _M_OFF0 = 0, 1, 2, 3, 4, 5


def _dn_masks():
    c = DN_CHUNK
    r = np.arange(c)[:, None]
    s = np.arange(c)[None, :]
    ms = [np.where(r >= s, 0.0, -np.inf), np.where(r <= s, 0.0, -np.inf), (r != s) * 1.0, (r == s) * 1.0,
          (r // LEAF == s // LEAF) * 1.0]
    blk = LEAF
    while blk < c:
        ms.append(((r // (2 * blk) == s // (2 * blk)) & (r // blk != s // blk)) * 1.0)
        blk *= 2
    return jnp.asarray(np.stack(ms), f32)


def _block_rows(x, blk, parity):
    return jnp.concatenate([x[i * blk:(i + 1) * blk] for i in range(x.shape[0] // blk) if i % 2 == parity], axis=0)


def _merge_block_rows(base, rows, blk, parity, combine):
    out = []
    for i in range(base.shape[0] // blk):
        piece = base[i * blk:(i + 1) * blk]
        if i % 2 == parity:
            piece = combine(piece, rows[(i // 2) * blk:(i // 2 + 1) * blk])
        out.append(piece)
    return jnp.concatenate(out, axis=0)


def _tri_inverse(a_mats, lower, m_ref):
    n = a_mats[0].shape[1]
    leaf, eye = m_ref[_M_LEAF], m_ref[_M_EYE]
    ps = [-(a * leaf) for a in a_mats]
    xs = [eye + p for p in ps]
    ps = [_bdot(p, p) for p in ps]
    for _ in range(int(math.log2(LEAF)) - 2):
        pbs = [p.astype(bf16) for p in ps]
        rs = [jnp.dot(pb, jnp.concatenate([pb, x.astype(bf16)], axis=1), preferred_element_type=f32)
              for pb, x in zip(pbs, xs)]
        ps = [r[:, :n] for r in rs]
        xs = [x + r[:, n:] for x, r in zip(xs, rs)]
    ts = [x + _bdot(p, x) for p, x in zip(ps, xs)]
    parity = [1 if lo else 0 for lo in lower]
    blk = LEAF
    for lvl in range(m_ref.shape[0] - _M_OFF0):
        off = m_ref[_M_OFF0 + lvl]
        zeros = jnp.zeros((n, n), bf16)
        vs = [_bdot(_block_rows(a * off, blk, pr), t).astype(bf16) for a, t, pr in zip(a_mats, ts, parity)]
        v_full = [_merge_block_rows(zeros, v, blk, pr, lambda _, r: r) for v, pr in zip(vs, parity)]
        ds = [_bdot(_block_rows(t, blk, pr), vf) for t, vf, pr in zip(ts, v_full, parity)]
        ts = [_merge_block_rows(t, d, blk, pr, lambda p, r: p - r) for t, d, pr in zip(ts, ds, parity)]
        blk *= 2
    return ts


def _dn_kernel(qf_ref, kf_ref, vf_ref, ktf_ref, gcf_ref, grf_ref,
               qb_ref, kb_ref, vb_ref, ktb_ref, gcb_ref, grb_ref, m_ref,
               of_ref, ob_ref, st_s, *, heads, head_block, head_dim):
    c_len = DN_CHUNK
    hb = pl.program_id(1)

    @pl.when(pl.program_id(2) == 0)
    def _():
        st_s[...] = jnp.zeros_like(st_s)

    lane = lax.broadcasted_iota(jnp.int32, (c_len, LANES), 1)
    dirs = ((qf_ref, kf_ref, vf_ref, ktf_ref, gcf_ref, grf_ref, of_ref, _M_NEG_LO, c_len - 1),
            (qb_ref, kb_ref, vb_ref, ktb_ref, gcb_ref, grb_ref, ob_ref, _M_NEG_UP, 0))
    chains = [(d, hh) for d in range(2) for hh in range(head_block)]
    sls = [slice(hh * head_dim, (hh + 1) * head_dim) for _, hh in chains]

    q_c = [dirs[d][0][:, sl] for (d, _), sl in zip(chains, sls)]
    k_c = [dirs[d][1][:, sl] for (d, _), sl in zip(chains, sls)]
    kt_c = [dirs[d][3][sl, :] for (d, _), sl in zip(chains, sls)]
    states = [st_s[d, hh] for d, hh in chains]
    gram = [jnp.dot(jnp.concatenate([q, k], axis=0), kt, preferred_element_type=f32)
            for q, k, kt in zip(q_c, k_c, kt_c)]
    kqs = [jnp.dot(jnp.concatenate([k, q], axis=0), s.astype(bf16), preferred_element_type=f32)
           for q, k, s in zip(q_c, k_c, states)]

    gc, gr, br = [], [], []
    for d, hh in chains:
        head = hb * head_block + hh
        g_col = (2 + d) * heads + head
        gc.append(jnp.sum(jnp.where(lane == g_col, dirs[d][4][...], 0.0), axis=-1, keepdims=True))
        gr.append(dirs[d][5][0, pl.ds(g_col, 1), :])
        br.append(dirs[d][5][0, pl.ds(d * heads + head, 1), :])

    e = [jnp.exp(c - r + m_ref[dirs[d][7]]) * b for (d, _), c, r, b in zip(chains, gc, gr, br)]
    attn = [(g[:c_len] * ee).astype(bf16) for g, ee in zip(gram, e)]
    offdiag = m_ref[_M_OFFDIAG]
    tinv = _tri_inverse([g[c_len:] * ee * offdiag for g, ee in zip(gram, e)], [d == 0 for d, _ in chains], m_ref)

    eg = [jnp.exp(c) for c in gc]
    r = [dirs[d][2][:, sl].astype(f32) - ee * ks[:c_len] for (d, _), sl, ee, ks in zip(chains, sls, eg, kqs)]
    y = [jnp.dot(t.astype(bf16), rr.astype(bf16), preferred_element_type=f32).astype(bf16)
         for t, rr in zip(tinv, r)]
    g_end = [jnp.sum(jnp.where(lane[0:1, :] == dirs[d][8], r_, 0.0), axis=-1, keepdims=True)
             for (d, _), r_ in zip(chains, gr)]
    kt_dec = [(kt.astype(f32) * (jnp.exp(ge - r_) * b)).astype(bf16)
              for kt, ge, r_, b in zip(kt_c, g_end, gr, br)]
    ay = [jnp.dot(jnp.concatenate([a, kd], axis=0), yy, preferred_element_type=f32)
          for a, kd, yy in zip(attn, kt_dec, y)]
    for (d, hh), sl, ee, ks, ay_, ge, s in zip(chains, sls, eg, kqs, ay, g_end, states):
        dirs[d][6][:, sl] = (ee * ks[c_len:] + ay_[:c_len]).astype(dirs[d][6].dtype)
        st_s[d, hh] = jnp.exp(ge) * s + ay_[c_len:]


def _deltanet(q, k, v, kt, gcol, grow, seq, heads, head_dim):
    t = q.shape[0]
    nb = t // seq
    n = seq // DN_CHUNK
    hblk = _pick(heads, DN_HEAD_BLOCK, 1)
    w = hblk * head_dim
    fwd = lambda b, h, j: b * n + j
    bwd = lambda b, h, j: b * n + (n - 1 - j)

    def specs(pos):
        tok = pl.BlockSpec((DN_CHUNK, w), lambda b, h, j: (pos(b, h, j), h))
        return [tok, tok, tok,
                pl.BlockSpec((w, DN_CHUNK), lambda b, h, j: (h, pos(b, h, j))),
                pl.BlockSpec((DN_CHUNK, LANES), lambda b, h, j: (pos(b, h, j), 0)),
                pl.BlockSpec((1, LANES, DN_CHUNK), lambda b, h, j: (pos(b, h, j), 0, 0))]

    masks = _dn_masks()
    out_sds = jax.ShapeDtypeStruct((t, heads * head_dim), bf16)
    return pl.pallas_call(
        functools.partial(_dn_kernel, heads=heads, head_block=hblk, head_dim=head_dim),
        grid=(nb, heads // hblk, n),
        in_specs=specs(fwd) + specs(bwd) + [pl.BlockSpec(masks.shape, lambda b, h, j: (0, 0, 0))],
        out_specs=[pl.BlockSpec((DN_CHUNK, w), lambda b, h, j: (fwd(b, h, j), h)),
                   pl.BlockSpec((DN_CHUNK, w), lambda b, h, j: (bwd(b, h, j), h))],
        out_shape=[out_sds, out_sds],
        scratch_shapes=[pltpu.VMEM((2, hblk, head_dim, head_dim), f32)],
        compiler_params=_params(("parallel", "parallel", "arbitrary")),
        name="deltanet",
    )(q, k, v, kt, gcol, grow, q, k, v, kt, gcol, grow, masks)


def _mix_kernel(of_ref, ob_ref, z_ref, dng_ref, u_ref, v_ref, ws_ref, bias_ref, o_ref, *,
                dn_heads, dn_dim, sg_heads, sg_dim, sg_chunk):
    dnw = dn_heads * dn_dim
    for hh in range(dn_heads):
        sl = slice(hh * dn_dim, (hh + 1) * dn_dim)
        o = of_ref[:, sl].astype(f32) + ob_ref[:, sl].astype(f32)
        ms = jnp.mean(o * o, axis=-1, keepdims=True)
        o_ref[:, sl] = (o * lax.rsqrt(ms + NORM_EPS) * dng_ref[...] * z_ref[:, sl].astype(f32)).astype(o_ref.dtype)
    for c in range(o_ref.shape[0] // sg_chunk):
        rows = slice(c * sg_chunk, (c + 1) * sg_chunk)
        for g in range(sg_heads):
            sl = slice(g * sg_dim, (g + 1) * sg_dim)
            mixed = jnp.dot(ws_ref[g], v_ref[rows, sl], preferred_element_type=f32) + bias_ref[g]
            o_ref[rows, dnw + g * sg_dim:dnw + (g + 1) * sg_dim] = (u_ref[rows, sl].astype(f32) * mixed).astype(o_ref.dtype)


def _mixer_out(o_f, o_b, z, dn_norm_g, dn_heads, u_act, v_norm, ws, bias, sg_heads, sg_chunk):
    t, dnw = o_f.shape
    sgw = u_act.shape[1]
    tb = _pick(t, 2 * sg_chunk, sg_chunk)
    sg_dim = sgw // sg_heads
    return pl.pallas_call(
        functools.partial(_mix_kernel, dn_heads=dn_heads, dn_dim=dnw // dn_heads, sg_heads=sg_heads,
                          sg_dim=sg_dim, sg_chunk=sg_chunk),
        grid=(t // tb,),
        in_specs=[pl.BlockSpec((tb, dnw), lambda i: (i, 0)), pl.BlockSpec((tb, dnw), lambda i: (i, 0)),
                  pl.BlockSpec((tb, dnw), lambda i: (i, 0)),
                  pl.BlockSpec((1, dnw // dn_heads), lambda i: (0, 0)),
                  pl.BlockSpec((tb, sgw), lambda i: (i, 0)), pl.BlockSpec((tb, sgw), lambda i: (i, 0)),
                  pl.BlockSpec((sg_heads, sg_chunk, sg_chunk), lambda i: (0, 0, 0)),
                  pl.BlockSpec((sg_heads, sg_chunk, sg_dim), lambda i: (0, 0, 0))],
        out_specs=pl.BlockSpec((tb, dnw + sgw), lambda i: (i, 0)),
        out_shape=jax.ShapeDtypeStruct((t, dnw + sgw), bf16),
        compiler_params=_params(("parallel",)),
        name="mixer_out",
    )(o_f, o_b, z, dn_norm_g, u_act, v_norm, ws, bias)


def _up_kernel(x_ref, wg_ref, wu_ref, halo_ref, cw_ref, ss_ref, o_ref, xn_s, *, nsplit, row_chunk):
    @pl.when(pl.program_id(1) == 0)
    def _():
        for c in range(x_ref.shape[0] // row_chunk):
            rows = slice(c * row_chunk, (c + 1) * row_chunk)
            r = lax.rsqrt(ss_ref[rows, :] * (1.0 / x_ref.shape[1]) + NORM_EPS)
            r = jnp.concatenate([r] * (x_ref.shape[1] // LANES), axis=1)
            xn_s[rows, :] = (x_ref[rows, :].astype(f32) * r).astype(xn_s.dtype)

    x = xn_s[...]
    w = o_ref.shape[1] // nsplit
    for s in range(nsplit):
        cs = slice(s * w, (s + 1) * w)
        g = jnp.dot(x, wg_ref[:, cs], preferred_element_type=f32)
        u = jnp.dot(x, wu_ref[:, cs], preferred_element_type=f32)
        gp, gn = _shifted(g, halo_ref[0:1, cs], halo_ref[1:2, cs])
        y = gp * cw_ref[0:1, cs] + g * cw_ref[1:2, cs] + gn * cw_ref[2:3, cs]
        o_ref[:, cs] = (y * jax.nn.sigmoid(y) * u).astype(o_ref.dtype)


def _up_conv_gate(h, ss, w_up, conv_w, ffp, seq, tm=1024, tn=512):
    t, k = h.shape
    tm = _pick(seq, tm, BF16_SUBLANES)
    tn = _pick(ffp, tn, LANES)
    halo = _matmul(_tile_halo_rows(h, tm, seq), w_up, f32, scale=_tile_halo_rows(ss, tm, seq), n_cols=ffp,
                   name="up_halo")
    uoff = ffp // tn
    return pl.pallas_call(
        functools.partial(_up_kernel, nsplit=max(1, tn // 256), row_chunk=_pick(tm, 128, BF16_SUBLANES)),
        grid=(t // tm, ffp // tn),
        in_specs=[pl.BlockSpec((tm, k), lambda i, j: (i, 0)),
                  pl.BlockSpec((k, tn), lambda i, j: (0, j)),
                  pl.BlockSpec((k, tn), lambda i, j: (0, j + uoff)),
                  pl.BlockSpec((8, tn), lambda i, j: (i, j)),
                  pl.BlockSpec((3, tn), lambda i, j: (0, j)),
                  pl.BlockSpec((tm, LANES), lambda i, j: (i, 0))],
        out_specs=pl.BlockSpec((tm, tn), lambda i, j: (i, j)),
        out_shape=jax.ShapeDtypeStruct((t, ffp), bf16),
        scratch_shapes=[pltpu.VMEM((tm, k), bf16)],
        compiler_params=_params(("parallel", "arbitrary")),
        name="up_conv_gate",
    )(h, w_up, w_up, halo, conv_w, ss)


def _cast_kernel(x_ref, g_ref, o_ref, *, zero_blocks, axis):
    blk = pl.program_id(axis)
    is_zero = functools.reduce(jnp.logical_or, [blk == z for z in zero_blocks], False)
    x = x_ref[...]
    if g_ref is not None:
        x = x * jnp.concatenate([g_ref[...]] * (x.shape[1] // LANES), axis=1)
    o_ref[...] = jnp.where(is_zero, 0.0, x).astype(o_ref.dtype)


def _cast_pad(w, layer, axis, segments, pad_to=None, row_gain=None):
    _, rows, cols = w.shape
    blk = functools.reduce(math.gcd, [v for seg in segments for v in seg if v] + [1024])
    src_of, zero_blocks = [], []
    for start, length in segments:
        padded = _round_up(length, pad_to) if pad_to else length
        for b in range(padded // blk):
            if b * blk < length:
                src_of.append((start + b * blk) // blk)
            else:
                zero_blocks.append(len(src_of))
                src_of.append(src_of[-1])
    nblk = len(src_of)
    table = jnp.asarray(src_of, jnp.int32)
    if axis == 1:
        tr = _pick(rows, CAST_BLOCK_ELEMS // blk, 8)
        grid = (rows // tr, nblk)
        in_spec = pl.BlockSpec((None, tr, blk), lambda i, j, tab: (layer, i, tab[j]))
        out_spec = pl.BlockSpec((tr, blk), lambda i, j, tab: (i, j))
        out_shape = (rows, nblk * blk)
    else:
        grid = (1, nblk)
        in_spec = pl.BlockSpec((None, blk, cols), lambda i, j, tab: (layer, tab[j], 0))
        out_spec = pl.BlockSpec((blk, cols), lambda i, j, tab: (j, 0))
        out_shape = (nblk * blk, cols)

    in_specs, args = [in_spec], [table, w]
    if row_gain is not None:
        assert axis == 1
        in_specs.append(pl.BlockSpec((tr, LANES), lambda i, j, tab: (i, 0)))
        args.append(jnp.broadcast_to(row_gain.astype(f32)[:, None], (rows, LANES)))

    def body(tab_ref, x_ref, *rest):
        _cast_kernel(x_ref, rest[0] if row_gain is not None else None, rest[-1], zero_blocks=zero_blocks, axis=1)

    return pl.pallas_call(
        body,
        grid_spec=pltpu.PrefetchScalarGridSpec(num_scalar_prefetch=1, grid=grid, in_specs=in_specs,
                                               out_specs=out_spec),
        out_shape=jax.ShapeDtypeStruct(out_shape, bf16),
        compiler_params=_params(("parallel", "parallel")),
        name="cast_pad",
    )(*args)


def _cast_transpose_kernel(a_ref, b_ref, g_ref, o_ref):
    x = jnp.concatenate([a_ref[...], b_ref[...]], axis=0) * g_ref[...]
    o_ref[...] = x.T.astype(o_ref.dtype)


def _cast_transpose(w_t, layer, row_start, n_rows, col_gain):
    half = LANES // 2
    assert row_start % half == 0 and n_rows % LANES == 0
    k = w_t.shape[2]
    base = row_start // half
    piece = lambda d: pl.BlockSpec((None, half, k), lambda j: (layer, base + 2 * j + d, 0))
    return pl.pallas_call(
        _cast_transpose_kernel,
        grid=(n_rows // LANES,),
        in_specs=[piece(0), piece(1), pl.BlockSpec((1, k), lambda j: (0, 0))],
        out_specs=pl.BlockSpec((k, LANES), lambda j: (0, j)),
        out_shape=jax.ShapeDtypeStruct((k, n_rows), bf16),
        compiler_params=_params(("parallel",)),
        name="cast_transpose",
    )(w_t, w_t, col_gain.astype(f32).reshape(1, k))


def _prepare_weights(norm1_g, w_in, qkv_conv_w, a_log, dt_bias, dn_norm_g, sg_norm_g, sg_w, sg_b,
                     w_out, norm2_g, w_up, ffn_conv_w, w_down):
    depth = w_in.shape[0]
    heads = a_log.shape[-1]
    dnw = qkv_conv_w.shape[-1] // 3
    sgw = sg_norm_g.shape[-1]
    ff = ffn_conv_w.shape[-1]
    ffp = _round_up(ff, 1024)
    pad_f = ffp - ff
    g0 = 4 * dnw
    g1 = g0 + 4 * heads
    assert 4 * heads <= LANES
    sg_heads, sg_chunk = sg_w.shape[1], sg_w.shape[2]
    dims = dict(heads=heads, dnw=dnw, sgw=sgw, ffp=ffp, sg_heads=sg_heads, sg_chunk=sg_chunk)
    lead = jnp.zeros((2 * heads,), f32)
    tail = jnp.zeros((LANES - 4 * heads,), f32)
    layers = []
    w_in_t = jnp.swapaxes(w_in, 1, 2)
    for l in range(depth):
        neg_a = -jnp.exp(a_log[l].astype(f32)).reshape(2 * heads)
        dtb = dt_bias[l].astype(f32).reshape(2 * heads)
        layers.append(dict(
            w_qkvz=_cast_transpose(w_in_t, l, 0, g0, norm1_g[l]),
            w_sg=_cast_transpose(w_in_t, l, g1, 2 * sgw, norm1_g[l]),
            w_gate=_cast_transpose(w_in_t, l, g0, LANES, norm1_g[l]),
            qkv_cw=qkv_conv_w[l].astype(f32),
            prm=jnp.stack([jnp.concatenate([lead, neg_a, tail]), jnp.concatenate([lead, dtb, tail])]),
            dn_norm_g=dn_norm_g[l].astype(f32).reshape(1, -1),
            sg_norm_g=sg_norm_g[l].astype(f32).reshape(1, -1),
            sg_w=sg_w[l].astype(bf16),
            sg_bias=jnp.broadcast_to(sg_b[l].astype(f32)[..., None], (sg_heads, sg_chunk, sgw // sg_heads)),
            w_out=_cast_pad(w_out, l, 1, [(0, w_out.shape[2])]),
            w_up=_cast_pad(w_up, l, 1, [(0, ff), (ff, ff)], pad_to=1024, row_gain=norm2_g[l]),
            ffn_cw=jnp.pad(ffn_conv_w[l].astype(f32), ((0, 0), (0, pad_f))),
            w_down=_cast_pad(w_down, l, 0, [(0, ff)], pad_to=1024),
        ))
    return layers, dims


def _trunk(x, layers, dims, final_norm_g):
    nb, seq, d = x.shape
    heads, dnw, sgw, ffp = dims["heads"], dims["dnw"], dims["sgw"], dims["ffp"]
    head_dim = dnw // heads
    xt = x.reshape(nb * seq, d)
    xb, ss = _rowstats(xt)
    tm = _pick(seq, 1024, BF16_SUBLANES)
    for w in layers:
        gates = _matmul(xb, w["w_gate"], f32, scale=ss, name="gate_proj")
        halo = _matmul(_tile_halo_rows(xb, tm, seq), w["w_qkvz"], f32, scale=_tile_halo_rows(ss, tm, seq),
                       n_cols=3 * dnw, name="qkv_halo")
        cw = w["qkv_cw"]
        (q,) = _proj_conv(xb, ss, w["w_qkvz"], halo, cw[:, :dnw], 0, dnw, tm, head_dim, True, head_dim ** -0.5,
                          False)
        k, kt = _proj_conv(xb, ss, w["w_qkvz"], halo, cw[:, dnw:2 * dnw], dnw, dnw, tm, head_dim, True, 1.0, True)
        (v,) = _proj_conv(xb, ss, w["w_qkvz"], halo, cw[:, 2 * dnw:], 2 * dnw, dnw, tm, head_dim, False, 1.0,
                          False)
        z = _matmul(xb, w["w_qkvz"], bf16, scale=ss, act="silu", col_off=3 * dnw, name="z_proj")
        u_act = _matmul(xb, w["w_sg"], bf16, scale=ss, act="gelu", n_cols=sgw, name="sg_u_proj")
        v_norm = _matmul(xb, w["w_sg"], bf16, scale=ss, act="gelu_norm", gain=w["sg_norm_g"],
                         head_dim=sgw // dims["sg_heads"], col_off=sgw, name="sg_v_proj")
        gcol, grow = _gate_prep(gates, w["prm"], heads)
        o_f, o_b = _deltanet(q, k, v, kt, gcol, grow, seq, heads, head_dim)
        o = _mixer_out(o_f, o_b, z, w["dn_norm_g"], heads, u_act, v_norm, w["sg_w"], w["sg_bias"],
                       dims["sg_heads"], dims["sg_chunk"])
        xt, xb, ss = _matmul(o, w["w_out"], f32, res=xt, emit_stats=True, tn=512, name="out_proj")
        hid = _up_conv_gate(xb, ss, w["w_up"], w["ffn_cw"], ffp, seq)
        if w is layers[-1]:
            xt = _matmul(hid, w["w_down"], f32, res=xt, tk=2816, name="down_proj")
        else:
            xt, xb, ss = _matmul(hid, w["w_down"], f32, res=xt, emit_stats=True, tk=2816, name="down_proj")
    return _rmsnorm(xt, final_norm_g, f32).reshape(nb, seq, d)


def kernel(x_prompt, x_sample, norm1_g, w_in, qkv_conv_w, a_log, dt_bias, dn_norm_g, sg_norm_g, sg_w, sg_b, w_out, norm2_g, w_up, ffn_conv_w, w_down, final_norm_g):
    layers, dims = _prepare_weights(norm1_g, w_in, qkv_conv_w, a_log, dt_bias, dn_norm_g, sg_norm_g, sg_w, sg_b,
                                    w_out, norm2_g, w_up, ffn_conv_w, w_down)
    return (_trunk(x_prompt, layers, dims, final_norm_g), _trunk(x_sample, layers, dims, final_norm_g))
```

```python
import functools
import math

import jax
import jax.numpy as jnp
import numpy as np
from jax import lax
from jax.experimental import pallas as pl
from jax.experimental.pallas import tpu as pltpu

NORM_EPS = 1e-6
L2_EPS = 1e-6
LANES = 128
BF16_SUBLANES = 16
DN_CHUNK = 128
LEAF = 16
DN_HEAD_BLOCK = 16
VMEM_LIMIT = 56 * 1024 * 1024
CAST_BLOCK_ELEMS = 1024 * 1024

f32 = jnp.float32
bf16 = jnp.bfloat16


def _pick(n, target, mult):
    best = None
    d = mult
    while d <= min(n, target):
        if n % d == 0:
            best = d
        d += mult
    return best if best is not None else n


def _round_up(n, m):
    return (n + m - 1) // m * m


def _params(sem):
    return pltpu.CompilerParams(dimension_semantics=sem, vmem_limit_bytes=VMEM_LIMIT)


def _bdot(a, b):
    return jnp.dot(a.astype(bf16), b.astype(bf16), preferred_element_type=f32)


def _rmsnorm_kernel(x_ref, g_ref, o_ref):
    x = x_ref[...]
    ms = jnp.mean(x * x, axis=-1, keepdims=True)
    o_ref[...] = (x * lax.rsqrt(ms + NORM_EPS) * g_ref[...]).astype(o_ref.dtype)


def _rmsnorm(x, g, out_dtype):
    t, d = x.shape
    tm = _pick(t, 256, 8)
    return pl.pallas_call(
        _rmsnorm_kernel,
        grid=(t // tm,),
        in_specs=[pl.BlockSpec((tm, d), lambda i: (i, 0)), pl.BlockSpec((1, d), lambda i: (0, 0))],
        out_specs=pl.BlockSpec((tm, d), lambda i: (i, 0)),
        out_shape=jax.ShapeDtypeStruct((t, d), out_dtype),
        compiler_params=_params(("parallel",)),
        name="rmsnorm",
    )(x, g.reshape(1, d).astype(f32))


def _rowstats_kernel(x_ref, xb_ref, ss_ref):
    x = x_ref[...]
    xb_ref[...] = x.astype(xb_ref.dtype)
    ss_ref[...] = jnp.broadcast_to(jnp.sum(x * x, axis=-1, keepdims=True), ss_ref.shape)


def _rowstats(x):
    t, d = x.shape
    tm = _pick(t, 256, 8)
    return pl.pallas_call(
        _rowstats_kernel,
        grid=(t // tm,),
        in_specs=[pl.BlockSpec((tm, d), lambda i: (i, 0))],
        out_specs=[pl.BlockSpec((tm, d), lambda i: (i, 0)), pl.BlockSpec((tm, LANES), lambda i: (i, 0))],
        out_shape=[jax.ShapeDtypeStruct((t, d), bf16), jax.ShapeDtypeStruct((t, LANES), f32)],
        compiler_params=_params(("parallel",)),
        name="rowstats",
    )(x)


def _row_scale(ss_ref, d, width):
    r = lax.rsqrt(ss_ref[...] * (1.0 / d) + NORM_EPS)
    return r if width == LANES else jnp.concatenate([r] * (width // LANES), axis=1)


def _mm_kernel(*refs, nk, has_res, has_scale, emit_stats, d_norm):
    refs = list(refs)
    a_ref, w_ref = refs[:2]
    pos = 2
    r_ref = s_ref = None
    if has_res:
        r_ref = refs[pos]
        pos += 1
    if has_scale:
        s_ref = refs[pos]
        pos += 1
    o_ref = refs[pos]
    pos += 1
    xb_ref = ss_ref = None
    if emit_stats:
        xb_ref, ss_ref = refs[pos:pos + 2]
        pos += 2
    scratch = refs[pos:]

    def finish(out):
        if s_ref is not None:
            out = out * _row_scale(s_ref, d_norm, out.shape[1])
        if r_ref is not None:
            out = out + r_ref[...]
        o_ref[...] = out.astype(o_ref.dtype)
        if emit_stats:
            xb_ref[...] = out.astype(xb_ref.dtype)
            part = jnp.broadcast_to(jnp.sum(out * out, axis=-1, keepdims=True), ss_ref.shape)
            j = pl.program_id(1)

            @pl.when(j == 0)
            def _():
                ss_ref[...] = part

            @pl.when(j > 0)
            def _():
                ss_ref[...] += part

    def dot():
        return jnp.dot(a_ref[...], w_ref[...], preferred_element_type=f32)

    if nk == 1:
        finish(dot())
        return
    acc_ref = scratch[0]
    k = pl.program_id(2)

    @pl.when(k == 0)
    def _():
        acc_ref[...] = dot()

    @pl.when(jnp.logical_and(k > 0, k < nk - 1))
    def _():
        acc_ref[...] += dot()

    @pl.when(k == nk - 1)
    def _():
        finish(acc_ref[...] + dot())


def _matmul(a, w, out_dtype, res=None, scale=None, emit_stats=False, tm=1024, tn=1024, tk=4096, col_off=0,
            n_cols=None, name="matmul"):
    m, k = a.shape
    n = w.shape[1] - col_off if n_cols is None else n_cols
    tm = _pick(m, tm, 8)
    tn = _pick(math.gcd(n, col_off) if col_off else n, tn, LANES)
    tk = _pick(k, tk, LANES)
    nk = k // tk
    off = col_off // tn
    in_specs = [pl.BlockSpec((tm, tk), lambda i, j, kk: (i, kk)),
                pl.BlockSpec((tk, tn), lambda i, j, kk: (kk, j + off))]
    args = [a, w]
    if res is not None:
        in_specs.append(pl.BlockSpec((tm, tn), lambda i, j, kk: (i, j)))
        args.append(res)
    if scale is not None:
        in_specs.append(pl.BlockSpec((tm, LANES), lambda i, j, kk: (i, 0)))
        args.append(scale)
    out_specs = [pl.BlockSpec((tm, tn), lambda i, j, kk: (i, j))]
    out_shape = [jax.ShapeDtypeStruct((m, n), out_dtype)]
    if emit_stats:
        out_specs += [pl.BlockSpec((tm, tn), lambda i, j, kk: (i, j)), pl.BlockSpec((tm, LANES), lambda i, j, kk: (i, 0))]
        out_shape += [jax.ShapeDtypeStruct((m, n), bf16), jax.ShapeDtypeStruct((m, LANES), f32)]
    scratch = [pltpu.VMEM((tm, tn), f32)] if nk > 1 else []
    outs = pl.pallas_call(
        functools.partial(_mm_kernel, nk=nk, has_res=res is not None, has_scale=scale is not None,
                          emit_stats=emit_stats, d_norm=k),
        grid=(m // tm, n // tn, nk),
        in_specs=in_specs,
        out_specs=out_specs,
        out_shape=out_shape,
        scratch_shapes=scratch,
        compiler_params=_params(("parallel", "arbitrary", "arbitrary")),
        name=name,
    )(*args)
    return outs if emit_stats else outs[0]


def _shifted(x, prev_row, next_row):
    ts = x.shape[0]
    rows = lax.broadcasted_iota(jnp.int32, x.shape, 0)
    xp = jnp.where(rows == 0, prev_row, pltpu.roll(x, 1, axis=0))
    xn = jnp.where(rows == ts - 1, next_row, pltpu.roll(x, ts - 1, axis=0))
    return xp, xn


def _tile_halo_rows(h, tm, seq):
    t, k = h.shape
    nt = t // tm
    h3 = h.reshape(nt, tm, k)
    zero = jnp.zeros((1, k), h.dtype)
    prev = jnp.concatenate([zero, h3[:-1, tm - 1]], axis=0)
    nxt = jnp.concatenate([h3[1:, 0], zero], axis=0)
    pos = jnp.arange(nt) % (seq // tm)
    prev = jnp.where((pos == 0)[:, None], jnp.zeros_like(prev), prev)
    nxt = jnp.where((pos == seq // tm - 1)[:, None], jnp.zeros_like(nxt), nxt)
    pad = jnp.zeros((nt, 6, k), h.dtype)
    return jnp.concatenate([prev[:, None], nxt[:, None], pad], axis=1).reshape(nt * 8, k)


def _proj_conv_kernel(x_ref, w_ref, halo_ref, cw_ref, ss_ref, *out_refs, nsplit, normalize, scale, head_dim):
    x = x_ref[...]
    o_ref = out_refs[0]
    w = o_ref.shape[1] // nsplit
    rs = _row_scale(ss_ref, x.shape[1], w)
    for s in range(nsplit):
        cs = slice(s * w, (s + 1) * w)
        pre = jnp.dot(x, w_ref[:, cs], preferred_element_type=f32) * rs
        pp, pn = _shifted(pre, halo_ref[0:1, cs], halo_ref[1:2, cs])
        y = pp * cw_ref[0:1, cs] + pre * cw_ref[1:2, cs] + pn * cw_ref[2:3, cs]
        y = y * jax.nn.sigmoid(y)
        for hh in range(w // head_dim):
            seg = y[:, hh * head_dim:(hh + 1) * head_dim]
            if normalize:
                ss = jnp.sum(seg * seg, axis=-1, keepdims=True)
                seg = seg * lax.rsqrt(ss + L2_EPS)
                if scale != 1.0:
                    seg = seg * scale
            hs = slice(s * w + hh * head_dim, s * w + (hh + 1) * head_dim)
            o_ref[:, hs] = seg.astype(o_ref.dtype)
            if len(out_refs) > 1:
                out_refs[1][hs, :] = seg.T.astype(out_refs[1].dtype)


def _proj_conv(h, ss, w, halo, conv_w, col_off, width, tm, head_dim, normalize, scale, transpose_out, tn=1024):
    t, k = h.shape
    tn = _pick(math.gcd(width, col_off) if col_off else width, tn, LANES)
    off = col_off // tn
    out_shape = [jax.ShapeDtypeStruct((t, width), bf16)]
    out_specs = [pl.BlockSpec((tm, tn), lambda i, j: (i, j))]
    if transpose_out:
        out_shape.append(jax.ShapeDtypeStruct((width, t), bf16))
        out_specs.append(pl.BlockSpec((tn, tm), lambda i, j: (j, i)))
    return pl.pallas_call(
        functools.partial(_proj_conv_kernel, nsplit=max(1, tn // 256), normalize=normalize, scale=scale,
                          head_dim=head_dim),
        grid=(t // tm, width // tn),
        in_specs=[pl.BlockSpec((tm, k), lambda i, j: (i, 0)),
                  pl.BlockSpec((k, tn), lambda i, j: (0, j + off)),
                  pl.BlockSpec((8, tn), lambda i, j: (i, j + off)),
                  pl.BlockSpec((3, tn), lambda i, j: (0, j)),
                  pl.BlockSpec((tm, LANES), lambda i, j: (i, 0))],
        out_specs=out_specs,
        out_shape=out_shape,
        compiler_params=_params(("parallel", "arbitrary")),
        name="proj_conv",
    )(h, w, halo, conv_w, ss)


def _gate_prep_kernel(gate_ref, prm_ref, gcol_ref, grow_ref, *, heads):
    c_len = DN_CHUNK
    row = lax.broadcasted_iota(jnp.int32, (c_len, c_len), 0)
    col = lax.broadcasted_iota(jnp.int32, (c_len, c_len), 1)
    tri_lo = (row >= col).astype(f32)
    tri_up = (row <= col).astype(f32)
    for c in range(gate_ref.shape[0] // c_len):
        rows = slice(c * c_len, (c + 1) * c_len)
        blk = gate_ref[rows, :]
        beta = jax.nn.sigmoid(blk)
        g = prm_ref[0:1, :] * jax.nn.softplus(blk + prm_ref[1:2, :])
        g_f = jnp.dot(tri_lo, g, preferred_element_type=f32, precision=lax.Precision.HIGHEST)
        g_b = jnp.dot(tri_up, g, preferred_element_type=f32, precision=lax.Precision.HIGHEST)
        out = jnp.where(col < 2 * heads, beta, jnp.where(col < 3 * heads, g_f, jnp.where(col < 4 * heads, g_b, 0.0)))
        gcol_ref[rows, :] = out
        grow_ref[c] = out.T


def _gate_prep(gates, prm, heads):
    t = gates.shape[0]
    n = t // DN_CHUNK
    cb = _pick(n, 8, 1)
    return pl.pallas_call(
        functools.partial(_gate_prep_kernel, heads=heads),
        grid=(n // cb,),
        in_specs=[pl.BlockSpec((cb * DN_CHUNK, LANES), lambda i: (i, 0)), pl.BlockSpec((2, LANES), lambda i: (0, 0))],
        out_specs=[pl.BlockSpec((cb * DN_CHUNK, LANES), lambda i: (i, 0)),
                   pl.BlockSpec((cb, LANES, DN_CHUNK), lambda i: (i, 0, 0))],
        out_shape=[jax.ShapeDtypeStruct((t, LANES), f32), jax.ShapeDtypeStruct((n, LANES, DN_CHUNK), f32)],
        compiler_params=_params(("parallel",)),
        name="gate_prep",
    )(gates, prm)


_M_NEG_LO, _M_NEG_UP, _M_OFFDIAG, _M_EYE, _M_LEAF, _M_OFF0 = 0, 1, 2, 3, 4, 5


def _dn_masks():
    c = DN_CHUNK
    r = np.arange(c)[:, None]
    s = np.arange(c)[None, :]
    ms = [np.where(r >= s, 0.0, -np.inf), np.where(r <= s, 0.0, -np.inf), (r != s) * 1.0, (r == s) * 1.0,
          (r // LEAF == s // LEAF) * 1.0]
    blk = LEAF
    while blk < c:
        ms.append(((r // (2 * blk) == s // (2 * blk)) & (r // blk != s // blk)) * 1.0)
        blk *= 2
    return jnp.asarray(np.stack(ms), f32)


def _block_rows(x, blk, parity):
    return jnp.concatenate([x[i * blk:(i + 1) * blk] for i in range(x.shape[0] // blk) if i % 2 == parity], axis=0)


def _merge_block_rows(base, rows, blk, parity, combine):
    out = []
    for i in range(base.shape[0] // blk):
        piece = base[i * blk:(i + 1) * blk]
        if i % 2 == parity:
            piece = combine(piece, rows[(i // 2) * blk:(i // 2 + 1) * blk])
        out.append(piece)
    return jnp.concatenate(out, axis=0)


def _tri_inverse(a_mats, lower, m_ref):
    n = a_mats[0].shape[1]
    leaf, eye = m_ref[_M_LEAF], m_ref[_M_EYE]
    ps = [-(a * leaf) for a in a_mats]
    xs = [eye + p for p in ps]
    ps = [_bdot(p, p) for p in ps]
    for _ in range(int(math.log2(LEAF)) - 2):
        pbs = [p.astype(bf16) for p in ps]
        rs = [jnp.dot(pb, jnp.concatenate([pb, x.astype(bf16)], axis=1), preferred_element_type=f32)
              for pb, x in zip(pbs, xs)]
        ps = [r[:, :n] for r in rs]
        xs = [x + r[:, n:] for x, r in zip(xs, rs)]
    ts = [x + _bdot(p, x) for p, x in zip(ps, xs)]
    parity = [1 if lo else 0 for lo in lower]
    blk = LEAF
    for lvl in range(m_ref.shape[0] - _M_OFF0):
        off = m_ref[_M_OFF0 + lvl]
        zeros = jnp.zeros((n, n), bf16)
        vs = [_bdot(_block_rows(a * off, blk, pr), t).astype(bf16) for a, t, pr in zip(a_mats, ts, parity)]
        v_full = [_merge_block_rows(zeros, v, blk, pr, lambda _, r: r) for v, pr in zip(vs, parity)]
        ds = [_bdot(_block_rows(t, blk, pr), vf) for t, vf, pr in zip(ts, v_full, parity)]
        ts = [_merge_block_rows(t, d, blk, pr, lambda p, r: p - r) for t, d, pr in zip(ts, ds, parity)]
        blk *= 2
    return ts


def _dn_kernel(qf_ref, kf_ref, vf_ref, ktf_ref, gcf_ref, grf_ref,
               qb_ref, kb_ref, vb_ref, ktb_ref, gcb_ref, grb_ref, m_ref,
               of_ref, ob_ref, st_s, *, heads, head_block, head_dim):
    c_len = DN_CHUNK
    hb = pl.program_id(1)

    @pl.when(pl.program_id(2) == 0)
    def _():
        st_s[...] = jnp.zeros_like(st_s)

    lane = lax.broadcasted_iota(jnp.int32, (c_len, LANES), 1)
    dirs = ((qf_ref, kf_ref, vf_ref, ktf_ref, gcf_ref, grf_ref, of_ref, _M_NEG_LO, c_len - 1),
            (qb_ref, kb_ref, vb_ref, ktb_ref, gcb_ref, grb_ref, ob_ref, _M_NEG_UP, 0))
    chains = [(d, hh) for d in range(2) for hh in range(head_block)]
    sls = [slice(hh * head_dim, (hh + 1) * head_dim) for _, hh in chains]

    q_c = [dirs[d][0][:, sl] for (d, _), sl in zip(chains, sls)]
    k_c = [dirs[d][1][:, sl] for (d, _), sl in zip(chains, sls)]
    kt_c = [dirs[d][3][sl, :] for (d, _), sl in zip(chains, sls)]
    states = [st_s[d, hh] for d, hh in chains]
    gram = [jnp.dot(jnp.concatenate([q, k], axis=0), kt, preferred_element_type=f32)
            for q, k, kt in zip(q_c, k_c, kt_c)]
    kqs = [jnp.dot(jnp.concatenate([k, q], axis=0), s.astype(bf16), preferred_element_type=f32)
           for q, k, s in zip(q_c, k_c, states)]

    gc, gr, br = [], [], []
    for d, hh in chains:
        head = hb * head_block + hh
        g_col = (2 + d) * heads + head
        gc.append(jnp.sum(jnp.where(lane == g_col, dirs[d][4][...], 0.0), axis=-1, keepdims=True))
        gr.append(dirs[d][5][0, pl.ds(g_col, 1), :])
        br.append(dirs[d][5][0, pl.ds(d * heads + head, 1), :])

    e = [jnp.exp(c - r + m_ref[dirs[d][7]]) * b for (d, _), c, r, b in zip(chains, gc, gr, br)]
    attn = [(g[:c_len] * ee).astype(bf16) for g, ee in zip(gram, e)]
    offdiag = m_ref[_M_OFFDIAG]
    tinv = _tri_inverse([g[c_len:] * ee * offdiag for g, ee in zip(gram, e)], [d == 0 for d, _ in chains], m_ref)

    eg = [jnp.exp(c) for c in gc]
    r = [dirs[d][2][:, sl].astype(f32) - ee * ks[:c_len] for (d, _), sl, ee, ks in zip(chains, sls, eg, kqs)]
    y = [jnp.dot(t.astype(bf16), rr.astype(bf16), preferred_element_type=f32).astype(bf16)
         for t, rr in zip(tinv, r)]
    g_end = [jnp.sum(jnp.where(lane[0:1, :] == dirs[d][8], r_, 0.0), axis=-1, keepdims=True)
             for (d, _), r_ in zip(chains, gr)]
    kt_dec = [(kt.astype(f32) * (jnp.exp(ge - r_) * b)).astype(bf16)
              for kt, ge, r_, b in zip(kt_c, g_end, gr, br)]
    ay = [jnp.dot(jnp.concatenate([a, kd], axis=0), yy, preferred_element_type=f32)
          for a, kd, yy in zip(attn, kt_dec, y)]
    for (d, hh), sl, ee, ks, ay_, ge, s in zip(chains, sls, eg, kqs, ay, g_end, states):
        dirs[d][6][:, sl] = (ee * ks[c_len:] + ay_[:c_len]).astype(dirs[d][6].dtype)
        st_s[d, hh] = jnp.exp(ge) * s + ay_[c_len:]


def _deltanet(q, k, v, kt, gcol, grow, seq, heads, head_dim):
    t = q.shape[0]
    nb = t // seq
    n = seq // DN_CHUNK
    hblk = _pick(heads, DN_HEAD_BLOCK, 1)
    w = hblk * head_dim
    fwd = lambda b, h, j: b * n + j
    bwd = lambda b, h, j: b * n + (n - 1 - j)

    def specs(pos):
        tok = pl.BlockSpec((DN_CHUNK, w), lambda b, h, j: (pos(b, h, j), h))
        return [tok, tok, tok,
                pl.BlockSpec((w, DN_CHUNK), lambda b, h, j: (h, pos(b, h, j))),
                pl.BlockSpec((DN_CHUNK, LANES), lambda b, h, j: (pos(b, h, j), 0)),
                pl.BlockSpec((1, LANES, DN_CHUNK), lambda b, h, j: (pos(b, h, j), 0, 0))]

    masks = _dn_masks()
    out_sds = jax.ShapeDtypeStruct((t, heads * head_dim), bf16)
    return pl.pallas_call(
        functools.partial(_dn_kernel, heads=heads, head_block=hblk, head_dim=head_dim),
        grid=(nb, heads // hblk, n),
        in_specs=specs(fwd) + specs(bwd) + [pl.BlockSpec(masks.shape, lambda b, h, j: (0, 0, 0))],
        out_specs=[pl.BlockSpec((DN_CHUNK, w), lambda b, h, j: (fwd(b, h, j), h)),
                   pl.BlockSpec((DN_CHUNK, w), lambda b, h, j: (bwd(b, h, j), h))],
        out_shape=[out_sds, out_sds],
        scratch_shapes=[pltpu.VMEM((2, hblk, head_dim, head_dim), f32)],
        compiler_params=_params(("parallel", "parallel", "arbitrary")),
        name="deltanet",
    )(q, k, v, kt, gcol, grow, q, k, v, kt, gcol, grow, masks)


def _mix_kernel(of_ref, ob_ref, z_ref, dng_ref, u_ref, v_ref, ws_ref, bias_ref, sgg_ref, o_ref, *,
                dn_heads, dn_dim, sg_heads, sg_dim, sg_chunk):
    dnw = dn_heads * dn_dim
    for hh in range(dn_heads):
        sl = slice(hh * dn_dim, (hh + 1) * dn_dim)
        o = of_ref[:, sl].astype(f32) + ob_ref[:, sl].astype(f32)
        ms = jnp.mean(o * o, axis=-1, keepdims=True)
        z = z_ref[:, sl].astype(f32)
        o_ref[:, sl] = (o * lax.rsqrt(ms + NORM_EPS) * dng_ref[...] * (z * jax.nn.sigmoid(z))).astype(o_ref.dtype)
    for c in range(o_ref.shape[0] // sg_chunk):
        rows = slice(c * sg_chunk, (c + 1) * sg_chunk)
        for g in range(sg_heads):
            sl = slice(g * sg_dim, (g + 1) * sg_dim)
            v = jax.nn.gelu(v_ref[rows, sl].astype(f32))
            ms = jnp.mean(v * v, axis=-1, keepdims=True)
            vn = v * lax.rsqrt(ms + NORM_EPS) * sgg_ref[:, sl]
            mixed = jnp.dot(ws_ref[g], vn.astype(bf16), preferred_element_type=f32) + bias_ref[g]
            u = jax.nn.gelu(u_ref[rows, sl].astype(f32))
            o_ref[rows, dnw + g * sg_dim:dnw + (g + 1) * sg_dim] = (u * mixed).astype(o_ref.dtype)


def _mixer_out(o_f, o_b, z, dn_norm_g, dn_heads, p_sg, sgw, ws, bias, sg_norm_g, sg_heads, sg_chunk):
    t, dnw = o_f.shape
    tb = _pick(t, 2 * sg_chunk, sg_chunk)
    sg_dim = sgw // sg_heads
    return pl.pallas_call(
        functools.partial(_mix_kernel, dn_heads=dn_heads, dn_dim=dnw // dn_heads, sg_heads=sg_heads,
                          sg_dim=sg_dim, sg_chunk=sg_chunk),
        grid=(t // tb,),
        in_specs=[pl.BlockSpec((tb, dnw), lambda i: (i, 0)), pl.BlockSpec((tb, dnw), lambda i: (i, 0)),
                  pl.BlockSpec((tb, dnw), lambda i: (i, 0)),
                  pl.BlockSpec((1, dnw // dn_heads), lambda i: (0, 0)),
                  pl.BlockSpec((tb, sgw), lambda i: (i, 0)), pl.BlockSpec((tb, sgw), lambda i: (i, 1)),
                  pl.BlockSpec((sg_heads, sg_chunk, sg_chunk), lambda i: (0, 0, 0)),
                  pl.BlockSpec((sg_heads, sg_chunk, sg_dim), lambda i: (0, 0, 0)),
                  pl.BlockSpec((1, sgw), lambda i: (0, 0))],
        out_specs=pl.BlockSpec((tb, dnw + sgw), lambda i: (i, 0)),
        out_shape=jax.ShapeDtypeStruct((t, dnw + sgw), bf16),
        compiler_params=_params(("parallel",)),
        name="mixer_out",
    )(o_f, o_b, z, dn_norm_g, p_sg, p_sg, ws, bias, sg_norm_g)


def _up_kernel(x_ref, wg_ref, wu_ref, halo_ref, cw_ref, ss_ref, o_ref, xn_s, *, nsplit, row_chunk):
    @pl.when(pl.program_id(1) == 0)
    def _():
        for c in range(x_ref.shape[0] // row_chunk):
            rows = slice(c * row_chunk, (c + 1) * row_chunk)
            r = lax.rsqrt(ss_ref[rows, :] * (1.0 / x_ref.shape[1]) + NORM_EPS)
            r = jnp.concatenate([r] * (x_ref.shape[1] // LANES), axis=1)
            xn_s[rows, :] = (x_ref[rows, :].astype(f32) * r).astype(xn_s.dtype)

    x = xn_s[...]
    w = o_ref.shape[1] // nsplit
    for s in range(nsplit):
        cs = slice(s * w, (s + 1) * w)
        g = jnp.dot(x, wg_ref[:, cs], preferred_element_type=f32)
        u = jnp.dot(x, wu_ref[:, cs], preferred_element_type=f32)
        gp, gn = _shifted(g, halo_ref[0:1, cs], halo_ref[1:2, cs])
        y = gp * cw_ref[0:1, cs] + g * cw_ref[1:2, cs] + gn * cw_ref[2:3, cs]
        o_ref[:, cs] = (y * jax.nn.sigmoid(y) * u).astype(o_ref.dtype)


def _up_conv_gate(h, ss, w_up, conv_w, ffp, seq, tm=1024, tn=512):
    t, k = h.shape
    tm = _pick(seq, tm, BF16_SUBLANES)
    tn = _pick(ffp, tn, LANES)
    halo = _matmul(_tile_halo_rows(h, tm, seq), w_up, f32, scale=_tile_halo_rows(ss, tm, seq), n_cols=ffp,
                   name="up_halo")
    uoff = ffp // tn
    return pl.pallas_call(
        functools.partial(_up_kernel, nsplit=max(1, tn // 256), row_chunk=_pick(tm, 128, BF16_SUBLANES)),
        grid=(t // tm, ffp // tn),
        in_specs=[pl.BlockSpec((tm, k), lambda i, j: (i, 0)),
                  pl.BlockSpec((k, tn), lambda i, j: (0, j)),
                  pl.BlockSpec((k, tn), lambda i, j: (0, j + uoff)),
                  pl.BlockSpec((8, tn), lambda i, j: (i, j)),
                  pl.BlockSpec((3, tn), lambda i, j: (0, j)),
                  pl.BlockSpec((tm, LANES), lambda i, j: (i, 0))],
        out_specs=pl.BlockSpec((tm, tn), lambda i, j: (i, j)),
        out_shape=jax.ShapeDtypeStruct((t, ffp), bf16),
        scratch_shapes=[pltpu.VMEM((tm, k), bf16)],
        compiler_params=_params(("parallel", "arbitrary")),
        name="up_conv_gate",
    )(h, w_up, w_up, halo, conv_w, ss)


def _cast_kernel(x_ref, g_ref, o_ref, *, zero_blocks, axis):
    blk = pl.program_id(axis)
    is_zero = functools.reduce(jnp.logical_or, [blk == z for z in zero_blocks], False)
    x = x_ref[...]
    if g_ref is not None:
        x = x * jnp.concatenate([g_ref[...]] * (x.shape[1] // LANES), axis=1)
    o_ref[...] = jnp.where(is_zero, 0.0, x).astype(o_ref.dtype)


def _cast_pad(w, layer, axis, segments, pad_to=None, row_gain=None):
    _, rows, cols = w.shape
    blk = functools.reduce(math.gcd, [v for seg in segments for v in seg if v] + [1024])
    src_of, zero_blocks = [], []
    for start, length in segments:
        padded = _round_up(length, pad_to) if pad_to else length
        for b in range(padded // blk):
            if b * blk < length:
                src_of.append((start + b * blk) // blk)
            else:
                zero_blocks.append(len(src_of))
                src_of.append(src_of[-1])
    nblk = len(src_of)
    table = jnp.asarray(src_of, jnp.int32)
    if axis == 1:
        tr = _pick(rows, CAST_BLOCK_ELEMS // blk, 8)
        grid = (rows // tr, nblk)
        in_spec = pl.BlockSpec((None, tr, blk), lambda i, j, tab: (layer, i, tab[j]))
        out_spec = pl.BlockSpec((tr, blk), lambda i, j, tab: (i, j))
        out_shape = (rows, nblk * blk)
    else:
        grid = (1, nblk)
        in_spec = pl.BlockSpec((None, blk, cols), lambda i, j, tab: (layer, tab[j], 0))
        out_spec = pl.BlockSpec((blk, cols), lambda i, j, tab: (j, 0))
        out_shape = (nblk * blk, cols)

    in_specs, args = [in_spec], [table, w]
    if row_gain is not None:
        assert axis == 1
        in_specs.append(pl.BlockSpec((tr, LANES), lambda i, j, tab: (i, 0)))
        args.append(jnp.broadcast_to(row_gain.astype(f32)[:, None], (rows, LANES)))

    def body(tab_ref, x_ref, *rest):
        _cast_kernel(x_ref, rest[0] if row_gain is not None else None, rest[-1], zero_blocks=zero_blocks, axis=1)

    return pl.pallas_call(
        body,
        grid_spec=pltpu.PrefetchScalarGridSpec(num_scalar_prefetch=1, grid=grid, in_specs=in_specs,
                                               out_specs=out_spec),
        out_shape=jax.ShapeDtypeStruct(out_shape, bf16),
        compiler_params=_params(("parallel", "parallel")),
        name="cast_pad",
    )(*args)


def _cast_transpose_kernel(a_ref, b_ref, g_ref, o_ref):
    x = jnp.concatenate([a_ref[...], b_ref[...]], axis=0) * g_ref[...]
    o_ref[...] = x.T.astype(o_ref.dtype)


def _cast_transpose(w_t, layer, row_start, n_rows, col_gain):
    half = LANES // 2
    assert row_start % half == 0 and n_rows % LANES == 0
    k = w_t.shape[2]
    base = row_start // half
    piece = lambda d: pl.BlockSpec((None, half, k), lambda j: (layer, base + 2 * j + d, 0))
    return pl.pallas_call(
        _cast_transpose_kernel,
        grid=(n_rows // LANES,),
        in_specs=[piece(0), piece(1), pl.BlockSpec((1, k), lambda j: (0, 0))],
        out_specs=pl.BlockSpec((k, LANES), lambda j: (0, j)),
        out_shape=jax.ShapeDtypeStruct((k, n_rows), bf16),
        compiler_params=_params(("parallel",)),
        name="cast_transpose",
    )(w_t, w_t, col_gain.astype(f32).reshape(1, k))


def _prepare_weights(norm1_g, w_in, qkv_conv_w, a_log, dt_bias, dn_norm_g, sg_norm_g, sg_w, sg_b,
                     w_out, norm2_g, w_up, ffn_conv_w, w_down):
    depth = w_in.shape[0]
    heads = a_log.shape[-1]
    dnw = qkv_conv_w.shape[-1] // 3
    sgw = sg_norm_g.shape[-1]
    ff = ffn_conv_w.shape[-1]
    ffp = _round_up(ff, 1024)
    pad_f = ffp - ff
    g0 = 4 * dnw
    g1 = g0 + 4 * heads
    assert 4 * heads <= LANES
    sg_heads, sg_chunk = sg_w.shape[1], sg_w.shape[2]
    dims = dict(heads=heads, dnw=dnw, sgw=sgw, ffp=ffp, sg_heads=sg_heads, sg_chunk=sg_chunk)
    lead = jnp.zeros((2 * heads,), f32)
    tail = jnp.zeros((LANES - 4 * heads,), f32)
    layers = []
    w_in_t = jnp.swapaxes(w_in, 1, 2)
    for l in range(depth):
        neg_a = -jnp.exp(a_log[l].astype(f32)).reshape(2 * heads)
        dtb = dt_bias[l].astype(f32).reshape(2 * heads)
        layers.append(dict(
            w_qkvz=_cast_transpose(w_in_t, l, 0, g0, norm1_g[l]),
            w_sg=_cast_transpose(w_in_t, l, g1, 2 * sgw, norm1_g[l]),
            w_gate=_cast_transpose(w_in_t, l, g0, LANES, norm1_g[l]),
            qkv_cw=qkv_conv_w[l].astype(f32),
            prm=jnp.stack([jnp.concatenate([lead, neg_a, tail]), jnp.concatenate([lead, dtb, tail])]),
            dn_norm_g=dn_norm_g[l].astype(f32).reshape(1, -1),
            sg_norm_g=sg_norm_g[l].astype(f32).reshape(1, -1),
            sg_w=sg_w[l].astype(bf16),
            sg_bias=jnp.broadcast_to(sg_b[l].astype(f32)[..., None], (sg_heads, sg_chunk, sgw // sg_heads)),
            w_out=_cast_pad(w_out, l, 1, [(0, w_out.shape[2])]),
            w_up=_cast_pad(w_up, l, 1, [(0, ff), (ff, ff)], pad_to=1024, row_gain=norm2_g[l]),
            ffn_cw=jnp.pad(ffn_conv_w[l].astype(f32), ((0, 0), (0, pad_f))),
            w_down=_cast_pad(w_down, l, 0, [(0, ff)], pad_to=1024),
        ))
    return layers, dims


def _trunk(x, layers, dims, final_norm_g):
    nb, seq, d = x.shape
    heads, dnw, sgw, ffp = dims["heads"], dims["dnw"], dims["sgw"], dims["ffp"]
    head_dim = dnw // heads
    xt = x.reshape(nb * seq, d)
    xb, ss = _rowstats(xt)
    tm = _pick(seq, 1024, BF16_SUBLANES)
    for w in layers:
        gates = _matmul(xb, w["w_gate"], f32, scale=ss, name="gate_proj")
        halo = _matmul(_tile_halo_rows(xb, tm, seq), w["w_qkvz"], f32, scale=_tile_halo_rows(ss, tm, seq),
                       n_cols=3 * dnw, name="qkv_halo")
        cw = w["qkv_cw"]
        (q,) = _proj_conv(xb, ss, w["w_qkvz"], halo, cw[:, :dnw], 0, dnw, tm, head_dim, True, head_dim ** -0.5,
                          False)
        k, kt = _proj_conv(xb, ss, w["w_qkvz"], halo, cw[:, dnw:2 * dnw], dnw, dnw, tm, head_dim, True, 1.0, True)
        (v,) = _proj_conv(xb, ss, w["w_qkvz"], halo, cw[:, 2 * dnw:], 2 * dnw, dnw, tm, head_dim, False, 1.0,
                          False)
        z = _matmul(xb, w["w_qkvz"], bf16, scale=ss, col_off=3 * dnw, name="z_proj")
        p_sg = _matmul(xb, w["w_sg"], bf16, scale=ss, name="sg_proj")
        gcol, grow = _gate_prep(gates, w["prm"], heads)
        o_f, o_b = _deltanet(q, k, v, kt, gcol, grow, seq, heads, head_dim)
        o = _mixer_out(o_f, o_b, z, w["dn_norm_g"], heads, p_sg, sgw,
                       w["sg_w"], w["sg_bias"], w["sg_norm_g"], dims["sg_heads"], dims["sg_chunk"])
        xt, xb, ss = _matmul(o, w["w_out"], f32, res=xt, emit_stats=True, tn=512, name="out_proj")
        hid = _up_conv_gate(xb, ss, w["w_up"], w["ffn_cw"], ffp, seq)
        if w is layers[-1]:
            xt = _matmul(hid, w["w_down"], f32, res=xt, tk=2816, name="down_proj")
        else:
            xt, xb, ss = _matmul(hid, w["w_down"], f32, res=xt, emit_stats=True, tk=2816, name="down_proj")
    return _rmsnorm(xt, final_norm_g, f32).reshape(nb, seq, d)


def kernel(x_prompt, x_sample, norm1_g, w_in, qkv_conv_w, a_log, dt_bias, dn_norm_g, sg_norm_g, sg_w, sg_b, w_out, norm2_g, w_up, ffn_conv_w, w_down, final_norm_g):
    layers, dims = _prepare_weights(norm1_g, w_in, qkv_conv_w, a_log, dt_bias, dn_norm_g, sg_norm_g, sg_w, sg_b,
                                    w_out, norm2_g, w_up, ffn_conv_w, w_down)
    return (_trunk(x_prompt, layers, dims, final_norm_g), _trunk(x_sample, layers, dims, final_norm_g))
```
